```python
import math
import jax, jax.numpy as jnp
from jax import lax
import numpy as np

D_MODEL = 2048
BATCH = 2
SEQ = 8192
DEPTH = 2

N_A_LAYERS = DEPTH // 2
N_B_LAYERS = DEPTH - N_A_LAYERS

S5_GROUP = 16
S5_GROUPS = D_MODEL // S5_GROUP
S5_STATE = 64

NSA_HEADS = 16
NSA_HEAD_DIM = D_MODEL // NSA_HEADS
NSA_KV_HEADS = 4
NSA_REP = NSA_HEADS // NSA_KV_HEADS
CMP_LEN = 32
CMP_STRIDE = 16
SEL_LEN = 64
SEL_TOPN = 16
WINDOW = 512
Q_BLOCK = 128
N_GATES = 3
N_KV_SLOTS = 6

MEM_TOKENS = 256
MEM_HEADS = 4
MEM_HEAD_DIM = D_MODEL // MEM_HEADS

D_FF = 4 * D_MODEL

DN_ALPHA = float((2 * DEPTH) ** 0.25)
DN_BETA = float((8 * DEPTH) ** -0.25)
LN_EPS = 1e-5
NEG_BIG = -1e30
SEL_FORCE = 1e9

kernel_name = 'hybrid_s5_nsa_yoco_deepnorm'


def layer_norm(x, g, b):
    xf = x.astype(jnp.float32)
    mu = jnp.mean(xf, axis=-1, keepdims=True)
    var = jnp.mean(jnp.square(xf - mu), axis=-1, keepdims=True)
    y = (xf - mu) * lax.rsqrt(var + LN_EPS)
    return (y * g.astype(jnp.float32) + b.astype(jnp.float32)).astype(x.dtype)


def post_norm(x, h, g, b):
    return layer_norm(DN_ALPHA * x + h, g, b)


def masked_softmax(s, mask):
    s = jnp.where(mask, s.astype(jnp.float32), NEG_BIG)
    m = jnp.max(s, axis=-1, keepdims=True)
    p = jnp.where(mask, jnp.exp(s - m), 0.0)
    return p / jnp.maximum(jnp.sum(p, axis=-1, keepdims=True), 1e-30)


def s5_mixer(x, w_in, a_re, a_im, log_dt, b_re, b_im, c_re, c_im, d_skip, w_glu, w_out):
    bsz, seq, _ = x.shape
    f32 = jnp.float32
    u = (x @ w_in).astype(f32).reshape(bsz, seq, S5_GROUPS, S5_GROUP)
    dt = jnp.exp(log_dt.astype(f32))[:, None]
    lr = a_re.astype(f32)
    li = a_im.astype(f32)
    mag = jnp.exp(lr * dt)
    ab_re = mag * jnp.cos(li * dt)
    ab_im = mag * jnp.sin(li * dt)
    den = lr * lr + li * li
    nr = ab_re - 1.0
    ni = ab_im
    f_re = (nr * lr + ni * li) / den
    f_im = (ni * lr - nr * li) / den
    br = b_re.astype(f32)
    bi = b_im.astype(f32)
    bb_re = f_re[..., None] * br - f_im[..., None] * bi
    bb_im = f_re[..., None] * bi + f_im[..., None] * br
    bu_re = jnp.einsum('blgh,gph->blgp', u, bb_re)
    bu_im = jnp.einsum('blgh,gph->blgp', u, bb_im)
    a_seq_re = jnp.broadcast_to(ab_re[None, None], (1, seq) + ab_re.shape)
    a_seq_im = jnp.broadcast_to(ab_im[None, None], (1, seq) + ab_im.shape)

    def combine(e_i, e_j):
        ar_i, ai_i, br_i, bi_i = e_i
        ar_j, ai_j, br_j, bi_j = e_j
        return (ar_j * ar_i - ai_j * ai_i,
                ar_j * ai_i + ai_j * ar_i,
                ar_j * br_i - ai_j * bi_i + br_j,
                ar_j * bi_i + ai_j * br_i + bi_j)

    _, _, s_re, s_im = lax.associative_scan(combine, (a_seq_re, a_seq_im, bu_re, bu_im), axis=1)
    y = (jnp.einsum('blgp,ghp->blgh', s_re, c_re.astype(f32))
         - jnp.einsum('blgp,ghp->blgh', s_im, c_im.astype(f32)))
    y = y + d_skip.astype(f32).reshape(S5_GROUPS, S5_GROUP) * u
    y = jax.nn.gelu(y.reshape(bsz, seq, D_MODEL)).astype(x.dtype)
    gl = y @ w_glu
    val, gate = jnp.split(gl, 2, axis=-1)
    return (val * jax.nn.sigmoid(gate)) @ w_out


def nsa_shared_kv(x, kv_w, cmp_pos_k, cmp_w1_k, cmp_w2_k, cmp_pos_v, cmp_w1_v, cmp_w2_v):
    bsz, seq, _ = x.shape
    kvh, hd = NSA_KV_HEADS, NSA_HEAD_DIM
    kv = (x @ kv_w).reshape(bsz, seq, N_KV_SLOTS, kvh, hd)
    k_c, v_c, k_s, v_s, k_w, v_w = [kv[:, :, i] for i in range(N_KV_SLOTS)]
    n_cmp = (seq - CMP_LEN) // CMP_STRIDE + 1
    idx = np.arange(n_cmp)[:, None] * CMP_STRIDE + np.arange(CMP_LEN)[None, :]

    def compress(t, pos, w1, w2):
        blk = t[:, idx] + pos[None, None, :, None, :]
        blk = jnp.moveaxis(blk, 3, 2).reshape(bsz, n_cmp, kvh, CMP_LEN * hd)
        return jax.nn.gelu(blk @ w1) @ w2

    k_cmp = compress(k_c, cmp_pos_k, cmp_w1_k, cmp_w2_k)
    v_cmp = compress(v_c, cmp_pos_v, cmp_w1_v, cmp_w2_v)
    n_sel = seq // SEL_LEN
    k_sel = k_s.reshape(bsz, n_sel, SEL_LEN, kvh, hd).transpose(0, 3, 1, 2, 4)
    v_sel = v_s.reshape(bsz, n_sel, SEL_LEN, kvh, hd).transpose(0, 3, 1, 2, 4)
    pad = ((0, 0), (WINDOW, 0), (0, 0), (0, 0))
    k_win = jnp.pad(k_w, pad)
    v_win = jnp.pad(v_w, pad)
    return k_cmp, v_cmp, k_sel, v_sel, k_win, v_win


def nsa_mixer(x, k_cmp, v_cmp, k_sel, v_sel, k_win, v_win, w_qg, w_o):
    bsz, seq, _ = x.shape
    H, G, R, hd = NSA_HEADS, NSA_KV_HEADS, NSA_REP, NSA_HEAD_DIM
    qg = x @ w_qg
    q = qg[..., :H * hd].reshape(bsz, seq, G, R, hd) * (hd ** -0.5)
    gates = jax.nn.sigmoid(qg[..., H * hd:].astype(jnp.float32)).reshape(bsz, seq, G, R, N_GATES)
    n_qb = seq // Q_BLOCK
    q_blocks = q.reshape(bsz, n_qb, Q_BLOCK, G, R, hd).swapaxes(0, 1)
    g_blocks = gates.reshape(bsz, n_qb, Q_BLOCK, G, R, N_GATES).swapaxes(0, 1)
    starts = jnp.arange(n_qb, dtype=jnp.int32) * Q_BLOCK

    n_cmp = k_cmp.shape[1]
    n_sel = k_sel.shape[2]
    top_n = min(SEL_TOPN, n_sel)
    cmp_end = jnp.arange(n_cmp, dtype=jnp.int32) * CMP_STRIDE + CMP_LEN - 1
    cs = np.arange(n_cmp) * CMP_STRIDE
    ss = np.arange(n_sel) * SEL_LEN
    sel_map = jnp.asarray(((cs[:, None] < ss[None, :] + SEL_LEN)
                           & (cs[:, None] + CMP_LEN > ss[None, :])).astype(np.float32))
    sel_ids = jnp.arange(n_sel, dtype=jnp.int32)
    b_ix = jnp.arange(bsz)[:, None, None, None]
    g_ix = jnp.arange(G)[None, :, None, None]
    win_off = jnp.arange(WINDOW + Q_BLOCK, dtype=jnp.int32) - WINDOW

    def one_block(args):
        qb, gb, s0 = args
        t = s0 + jnp.arange(Q_BLOCK, dtype=jnp.int32)
        sc = jnp.einsum('bqgrd,bcgd->bgrqc', qb, k_cmp)
        pc = masked_softmax(sc, cmp_end[None, :] <= t[:, None])
        o_cmp = jnp.einsum('bgrqc,bcgd->bqgrd', pc.astype(v_cmp.dtype), v_cmp)
        imp = jnp.einsum('bgrqc,cs->bgqs', pc, sel_map)
        cur = (t // SEL_LEN)[:, None]
        valid = sel_ids[None, :] * SEL_LEN <= t[:, None]
        forced = (sel_ids[None, :] == 0) | (sel_ids[None, :] == cur) | (sel_ids[None, :] == cur - 1)
        imp = jnp.where(forced, SEL_FORCE, jnp.where(valid, imp, -SEL_FORCE))
        _, sel_idx = lax.top_k(imp, top_n)
        ks = k_sel[b_ix, g_ix, sel_idx].reshape(bsz, G, Q_BLOCK, top_n * SEL_LEN, hd)
        vs = v_sel[b_ix, g_ix, sel_idx].reshape(bsz, G, Q_BLOCK, top_n * SEL_LEN, hd)
        kpos = (sel_idx[..., None] * SEL_LEN + jnp.arange(SEL_LEN, dtype=jnp.int32)).reshape(
            bsz, G, Q_BLOCK, top_n * SEL_LEN)
        s_sel = jnp.einsum('bqgrd,bgqkd->bgrqk', qb, ks)
        p_sel = masked_softmax(s_sel, (kpos <= t[None, None, :, None])[:, :, None])
        o_slc = jnp.einsum('bgrqk,bgqkd->bqgrd', p_sel.astype(vs.dtype), vs)
        kw = lax.dynamic_slice_in_dim(k_win, s0, WINDOW + Q_BLOCK, axis=1)
        vw = lax.dynamic_slice_in_dim(v_win, s0, WINDOW + Q_BLOCK, axis=1)
        kp = s0 + win_off
        mw = (kp[None, :] <= t[:, None]) & (kp[None, :] > t[:, None] - WINDOW) & (kp[None, :] >= 0)
        s_w = jnp.einsum('bqgrd,bkgd->bgrqk', qb, kw)
        p_w = masked_softmax(s_w, mw)
        o_win = jnp.einsum('bgrqk,bkgd->bqgrd', p_w.astype(vw.dtype), vw)
        out = gb[..., 0:1] * o_cmp + gb[..., 1:2] * o_slc + gb[..., 2:3] * o_win
        return out.astype(x.dtype)

    o = lax.map(one_block, (q_blocks, g_blocks, starts))
    o = o.swapaxes(0, 1).reshape(bsz, seq, H * hd)
    return o @ w_o


def memory_attention(x, mem, w_q, w_kv, w_o):
    bsz, seq, _ = x.shape
    n_mem = mem.shape[1]
    q = (x @ w_q).reshape(bsz, seq, MEM_HEADS, MEM_HEAD_DIM)
    kv = (mem @ w_kv).reshape(bsz, n_mem, 2, MEM_HEADS, MEM_HEAD_DIM)
    k = kv[:, :, 0]
    v = kv[:, :, 1]
    s = jnp.einsum('blhd,bmhd->bhlm', q, k).astype(jnp.float32) * (MEM_HEAD_DIM ** -0.5)
    p = jax.nn.softmax(s, axis=-1).astype(v.dtype)
    o = jnp.einsum('bhlm,bmhd->blhd', p, v).reshape(bsz, seq, D_MODEL)
    return o @ w_o


def sq_relu_mlp(x, w_up, w_down):
    return jnp.square(jax.nn.relu(x @ w_up)) @ w_down


def _normal(key, shape, scale):
    return jax.random.normal(key, shape, jnp.float32) * scale


def setup_inputs(seed: int = 0) -> dict:
    key = jax.random.key(seed)
    ks = jax.random.split(key, 29)
    nA, nB, D = N_A_LAYERS, N_B_LAYERS, D_MODEL
    G, P, GS = S5_GROUPS, S5_STATE, S5_GROUP
    H, hd, kvh = NSA_HEADS, NSA_HEAD_DIM, NSA_KV_HEADS
    n_idx = jnp.arange(P, dtype=jnp.float32)[None, None, :]
    inp = {}
    inp['x'] = _normal(ks[0], (BATCH, SEQ, D), 1.0)
    inp['mem'] = _normal(ks[1], (BATCH, MEM_TOKENS, D), 1.0)
    inp['s5_w_in'] = _normal(ks[2], (nA, D, D), D ** -0.5)
    inp['s5_a_re'] = -0.5 + _normal(ks[3], (nA, G, P), 0.01)
    inp['s5_a_im'] = math.pi * n_idx + _normal(ks[4], (nA, G, P), 0.01)
    inp['s5_log_dt'] = jax.random.uniform(ks[5], (nA, G), jnp.float32, math.log(1e-3), math.log(1e-1))
    inp['s5_b_re'] = _normal(ks[6], (nA, G, P, GS), (2 * GS) ** -0.5)
    inp['s5_b_im'] = _normal(ks[7], (nA, G, P, GS), (2 * GS) ** -0.5)
    inp['s5_c_re'] = _normal(ks[8], (nA, G, GS, P), P ** -0.5)
    inp['s5_c_im'] = _normal(ks[9], (nA, G, GS, P), P ** -0.5)
    inp['s5_d'] = _normal(ks[10], (nA, D), 1.0)
    inp['s5_w_glu'] = _normal(ks[11], (nA, D, 2 * D), D ** -0.5)
    inp['s5_w_out'] = _normal(ks[12], (nA, D, D), DN_BETA * D ** -0.5)
    inp['kv_w'] = _normal(ks[13], (D, N_KV_SLOTS * kvh * hd), D ** -0.5)
    inp['cmp_pos_k'] = _normal(ks[14], (CMP_LEN, hd), 0.02)
    inp['cmp_w1_k'] = _normal(ks[15], (CMP_LEN * hd, hd), (CMP_LEN * hd) ** -0.5)
    inp['cmp_w2_k'] = _normal(ks[16], (hd, hd), hd ** -0.5)
    inp['cmp_pos_v'] = _normal(ks[17], (CMP_LEN, hd), 0.02)
    inp['cmp_w1_v'] = _normal(ks[18], (CMP_LEN * hd, hd), (CMP_LEN * hd) ** -0.5)
    inp['cmp_w2_v'] = _normal(ks[19], (hd, hd), hd ** -0.5)
    inp['nsa_w_qg'] = _normal(ks[20], (nB, D, H * hd + N_GATES * H), D ** -0.5)
    inp['nsa_w_o'] = _normal(ks[21], (nB, H * hd, D), DN_BETA * (H * hd) ** -0.5)
    inp['mem_w_q'] = _normal(ks[22], (DEPTH, D, D), D ** -0.5)
    inp['mem_w_kv'] = _normal(ks[23], (DEPTH, D, 2 * D), D ** -0.5)
    inp['mem_w_o'] = _normal(ks[24], (DEPTH, D, D), DN_BETA * D ** -0.5)
    inp['mlp_w_up'] = _normal(ks[25], (DEPTH, D, D_FF), D ** -0.5)
    inp['mlp_w_down'] = _normal(ks[26], (DEPTH, D_FF, D), DN_BETA * D_FF ** -0.5)
    inp['ln_g'] = 1.0 + _normal(ks[27], (DEPTH, 3, D), 0.02)
    inp['ln_b'] = _normal(ks[28], (DEPTH, 3, D), 0.02)
    return inp


def reference(x, mem, s5_w_in, s5_a_re, s5_a_im, s5_log_dt, s5_b_re, s5_b_im, s5_c_re, s5_c_im, s5_d,
              s5_w_glu, s5_w_out, kv_w, cmp_pos_k, cmp_w1_k, cmp_w2_k, cmp_pos_v, cmp_w1_v, cmp_w2_v,
              nsa_w_qg, nsa_w_o, mem_w_q, mem_w_kv, mem_w_o, mlp_w_up, mlp_w_down, ln_g, ln_b):
    for layer in range(DEPTH):
        if layer < N_A_LAYERS:
            i = layer
            h = s5_mixer(x, s5_w_in[i], s5_a_re[i], s5_a_im[i], s5_log_dt[i], s5_b_re[i], s5_b_im[i],
                         s5_c_re[i], s5_c_im[i], s5_d[i], s5_w_glu[i], s5_w_out[i])
        else:
            i = layer - N_A_LAYERS
            if i == 0:
                k_cmp, v_cmp, k_sel, v_sel, k_win, v_win = nsa_shared_kv(
                    x, kv_w, cmp_pos_k, cmp_w1_k, cmp_w2_k, cmp_pos_v, cmp_w1_v, cmp_w2_v)
            h = nsa_mixer(x, k_cmp, v_cmp, k_sel, v_sel, k_win, v_win, nsa_w_qg[i], nsa_w_o[i])
        x = post_norm(x, h, ln_g[layer, 0], ln_b[layer, 0])
        x = post_norm(x, memory_attention(x, mem, mem_w_q[layer], mem_w_kv[layer], mem_w_o[layer]),
                      ln_g[layer, 1], ln_b[layer, 1])
        x = post_norm(x, sq_relu_mlp(x, mlp_w_up[layer], mlp_w_down[layer]), ln_g[layer, 2], ln_b[layer, 2])
    return x
```

```python
import functools
import math

import jax
import jax.numpy as jnp
import numpy as np
from jax import lax
from jax.experimental import pallas as pl
from jax.experimental.pallas import tpu as pltpu

F32 = jnp.float32
BF16 = jnp.bfloat16

D_MODEL = 2048
DEPTH = 2
N_A_LAYERS = DEPTH // 2

S5_GROUP = 16
S5_GROUPS = D_MODEL // S5_GROUP
S5_STATE = 64
S5_CHUNK = 16
S5_ROW = S5_CHUNK * S5_GROUP
S5_GROUP_BLOCK = 8

NSA_HEADS = 16
NSA_HEAD_DIM = D_MODEL // NSA_HEADS
NSA_KV_HEADS = 4
NSA_REP = NSA_HEADS // NSA_KV_HEADS
CMP_LEN = 32
CMP_STRIDE = 16
SEL_LEN = 64
SEL_TOPN = 16
WINDOW = 512
Q_BLOCK = 128
N_GATES = 3
N_KV_SLOTS = 6
SEL_TILE = 512

MEM_HEADS = 4
MEM_HEAD_DIM = D_MODEL // MEM_HEADS

D_FF = 4 * D_MODEL

DN_ALPHA = float((2 * DEPTH) ** 0.25)
LN_EPS = 1e-5
NEG_BIG = -1e30
SEL_FORCE = 1e9

LANES = 128
VMEM_LIMIT = 56 * 1024 * 1024


def _params(*semantics):
    return pltpu.CompilerParams(dimension_semantics=semantics, vmem_limit_bytes=VMEM_LIMIT)


def _layer_norm(y, g, b):
    mu = jnp.mean(y, axis=-1, keepdims=True)
    yc = y - mu
    var = jnp.mean(yc * yc, axis=-1, keepdims=True)
    return yc * lax.rsqrt(var + LN_EPS) * g + b


def _dot(a, b):
    return jnp.dot(a, b, preferred_element_type=F32)


def _dot_nt(a, b):
    return lax.dot_general(a, b, (((1,), (1,)), ((), ())), preferred_element_type=F32)


def _mm_kernel(x_ref, w_ref, o_ref, *, scale, act):
    acc = _dot(x_ref[...].astype(BF16), w_ref[...])
    if scale is not None:
        acc = acc * scale
    if act == "sigmoid":
        acc = jax.nn.sigmoid(acc)
    o_ref[...] = acc.astype(o_ref.dtype)


def _mm(x, w, out_dtype, *, tm, tn, scale=None, act=None):
    m, k = x.shape
    n = w.shape[1]
    tm, tn = min(tm, m), min(tn, n)
    return pl.pallas_call(
        functools.partial(_mm_kernel, scale=scale, act=act),
        grid=(m // tm, n // tn),
        in_specs=[pl.BlockSpec((tm, k), lambda i, j: (i, 0)),
                  pl.BlockSpec((k, tn), lambda i, j: (0, j))],
        out_specs=pl.BlockSpec((tm, tn), lambda i, j: (i, j)),
        out_shape=jax.ShapeDtypeStruct((m, n), out_dtype),
        compiler_params=_params("parallel", "arbitrary"),
        name="mm",
    )(x, w)


def _glu_kernel(y_ref, wv_ref, wg_ref, o_ref):
    y = y_ref[...]
    val = _dot(y, wv_ref[...])
    gate = _dot(y, wg_ref[...])
    o_ref[...] = (val * jax.nn.sigmoid(gate)).astype(o_ref.dtype)


def _mm_glu(y, w_glu, *, tm=1024, tn=512):
    m, k = y.shape
    n = w_glu.shape[1] // 2
    tm = min(tm, m)
    nb = n // tn
    return pl.pallas_call(
        _glu_kernel,
        grid=(m // tm, nb),
        in_specs=[pl.BlockSpec((tm, k), lambda i, j: (i, 0)),
                  pl.BlockSpec((k, tn), lambda i, j: (0, j)),
                  pl.BlockSpec((k, tn), lambda i, j: (0, j + nb))],
        out_specs=pl.BlockSpec((tm, tn), lambda i, j: (i, j)),
        out_shape=jax.ShapeDtypeStruct((m, n), BF16),
        compiler_params=_params("parallel", "arbitrary"),
        name="mm_glu",
    )(y, w_glu, w_glu)


def _res_ln_kernel(z_ref, w_ref, x_ref, g_ref, b_ref, o_ref, ob_ref):
    h = _dot(z_ref[...], w_ref[...])
    y = _layer_norm(DN_ALPHA * x_ref[...] + h, g_ref[...], b_ref[...])
    o_ref[...] = y
    ob_ref[...] = y.astype(BF16)


def _mm_res_ln(z, w, x, g, b, *, tm=512):
    m, k = z.shape
    d = w.shape[1]
    tm = min(tm, m)
    row = lambda i: (i, 0)
    fixed = lambda i: (0, 0)
    return pl.pallas_call(
        _res_ln_kernel,
        grid=(m // tm,),
        in_specs=[pl.BlockSpec((tm, k), row), pl.BlockSpec((k, d), fixed),
                  pl.BlockSpec((tm, d), row), pl.BlockSpec((1, d), fixed),
                  pl.BlockSpec((1, d), fixed)],
        out_specs=[pl.BlockSpec((tm, d), row), pl.BlockSpec((tm, d), row)],
        out_shape=[jax.ShapeDtypeStruct((m, d), F32), jax.ShapeDtypeStruct((m, d), BF16)],
        compiler_params=_params("parallel"),
        name="proj_postnorm",
    )(z, w, x, g.reshape(1, d), b.reshape(1, d))


def _mlp_kernel(xb_ref, x_ref, wu_ref, wd_ref, g_ref, b_ref, o_ref, ob_ref, acc_ref):
    j = pl.program_id(1)
    h = jnp.maximum(_dot(xb_ref[...], wu_ref[...]), 0.0)
    part = _dot((h * h).astype(BF16), wd_ref[...])

    @pl.when(j == 0)
    def _():
        acc_ref[...] = part

    @pl.when(j > 0)
    def _():
        acc_ref[...] += part

    @pl.when(j == pl.num_programs(1) - 1)
    def _():
        y = _layer_norm(DN_ALPHA * x_ref[...] + acc_ref[...], g_ref[...], b_ref[...])
        o_ref[...] = y
        ob_ref[...] = y.astype(BF16)


def _mlp_ln(xb, x, w_up, w_down, g, b, *, tm=512, tf=1024):
    m, d = x.shape
    ff = w_up.shape[1]
    tm = min(tm, m)
    row = lambda i, j: (i, 0)
    fixed = lambda i, j: (0, 0)
    return pl.pallas_call(
        _mlp_kernel,
        grid=(m // tm, ff // tf),
        in_specs=[pl.BlockSpec((tm, d), row), pl.BlockSpec((tm, d), row),
                  pl.BlockSpec((d, tf), lambda i, j: (0, j)),
                  pl.BlockSpec((tf, d), lambda i, j: (j, 0)),
                  pl.BlockSpec((1, d), fixed), pl.BlockSpec((1, d), fixed)],
        out_specs=[pl.BlockSpec((tm, d), row), pl.BlockSpec((tm, d), row)],
        out_shape=[jax.ShapeDtypeStruct((m, d), F32), jax.ShapeDtypeStruct((m, d), BF16)],
        scratch_shapes=[pltpu.VMEM((tm, d), F32)],
        compiler_params=_params("parallel", "arbitrary"),
        name="mlp_postnorm",
    )(xb, x, w_up, w_down, g.reshape(1, d), b.reshape(1, d))


def _mem_attn_kernel(x_ref, wq_ref, kv_ref, o_ref):
    q = _dot(x_ref[0], wq_ref[...]).astype(BF16)
    scale = MEM_HEAD_DIM ** -0.5
    for h in range(MEM_HEADS):
        lo = h * MEM_HEAD_DIM
        k = kv_ref[0, :, lo:lo + MEM_HEAD_DIM]
        v = kv_ref[0, :, D_MODEL + lo:D_MODEL + lo + MEM_HEAD_DIM]
        s = _dot_nt(q[:, lo:lo + MEM_HEAD_DIM], k) * scale
        p = jnp.exp(s - jnp.max(s, axis=-1, keepdims=True))
        p = p / jnp.sum(p, axis=-1, keepdims=True)
        o_ref[0, :, lo:lo + MEM_HEAD_DIM] = _dot(p.astype(BF16), v).astype(o_ref.dtype)


def _mem_attn(xb, wq, kvm, *, tm=512):
    bsz, seq, d = xb.shape
    n_mem = kvm.shape[1]
    tm = min(tm, seq)
    return pl.pallas_call(
        _mem_attn_kernel,
        grid=(bsz, seq // tm),
        in_specs=[pl.BlockSpec((1, tm, d), lambda b, i: (b, i, 0)),
                  pl.BlockSpec((d, d), lambda b, i: (0, 0)),
                  pl.BlockSpec((1, n_mem, 2 * d), lambda b, i: (b, 0, 0))],
        out_specs=pl.BlockSpec((1, tm, d), lambda b, i: (b, i, 0)),
        out_shape=jax.ShapeDtypeStruct((bsz, seq, d), BF16),
        compiler_params=_params("parallel", "arbitrary"),
        name="mem_attn",
    )(xb, wq, kvm)


def _s5_tables(a_re, a_im, log_dt, b_re, b_im, c_re, c_im, d_skip, n_chunks):
    hi = lax.Precision.HIGHEST
    G, P, GS, T = S5_GROUPS, S5_STATE, S5_GROUP, S5_CHUNK
    dt = jnp.exp(log_dt.astype(F32))[:, None]
    lr, li = a_re.astype(F32), a_im.astype(F32)
    mag = jnp.exp(lr * dt)
    ab_re, ab_im = mag * jnp.cos(li * dt), mag * jnp.sin(li * dt)
    den = lr * lr + li * li
    nr, ni = ab_re - 1.0, ab_im
    f_re = (nr * lr + ni * li) / den
    f_im = (ni * lr - nr * li) / den
    br, bi = b_re.astype(F32), b_im.astype(F32)
    bb_re = f_re[..., None] * br - f_im[..., None] * bi
    bb_im = f_re[..., None] * bi + f_im[..., None] * br

    def apow(n):
        n = jnp.asarray(n, F32)[:, None, None]
        m = jnp.exp(lr * dt * n)
        return m * jnp.cos(li * dt * n), m * jnp.sin(li * dt * n)

    pw_re, pw_im = apow(np.arange(T + 1))
    abb_re = pw_re[..., None] * bb_re - pw_im[..., None] * bb_im
    abb_im = pw_re[..., None] * bb_im + pw_im[..., None] * bb_re

    rev = np.arange(T - 1, -1, -1)
    w_inc = jnp.concatenate([abb_re[rev], abb_im[rev]], axis=2)
    w_inc = w_inc.transpose(1, 0, 3, 2).reshape(G, T * GS, 2 * P)

    cr, ci = c_re.astype(F32), c_im.astype(F32)
    kern = (jnp.einsum("ghp,ngpk->nghk", cr, abb_re[:T], precision=hi)
            - jnp.einsum("ghp,ngpk->nghk", ci, abb_im[:T], precision=hi))
    tau = np.arange(T)[None, :] - np.arange(T)[:, None]
    toep = jnp.where((tau >= 0)[:, :, None, None, None], kern[np.clip(tau, 0, T - 1)], 0.0)
    w_toep = toep.transpose(2, 0, 4, 1, 3).reshape(G, T * GS, T * GS)

    pr, pi = pw_re[1:], pw_im[1:]
    car_re = cr[None] * pr[:, :, None, :] - ci[None] * pi[:, :, None, :]
    car_im = cr[None] * pi[:, :, None, :] + ci[None] * pr[:, :, None, :]
    w_car = jnp.concatenate([car_re, -car_im], axis=3)
    w_car = w_car.transpose(1, 3, 0, 2).reshape(G, 2 * P, T * GS)

    n_steps = max(1, int(math.ceil(math.log2(n_chunks))))
    sr, si = apow(T * (2 ** np.arange(n_steps)))
    apow_a = jnp.concatenate([sr, sr], axis=2).transpose(1, 0, 2)
    apow_b = jnp.concatenate([-si, si], axis=2).transpose(1, 0, 2)
    dsk = jnp.tile(d_skip.astype(F32).reshape(G, 1, GS), (1, T, 1)).reshape(G, 1, T * GS)
    return w_inc.astype(BF16), w_toep.astype(BF16), w_car.astype(BF16), apow_a, apow_b, dsk


def _s5_kernel(u_ref, winc_ref, wtoep_ref, wcar_ref, pa_ref, pb_ref, d_ref, o_ref, *, n_steps):
    n_chunks = u_ref.shape[2]
    rows = lax.broadcasted_iota(jnp.int32, (n_chunks, 2 * S5_STATE), 0)
    for g in range(u_ref.shape[1]):
        u = u_ref[0, g]
        ub = u.astype(BF16)
        s = _dot(ub, winc_ref[g])
        for k in range(n_steps):
            d = 1 << k
            sh = jnp.where(rows >= d, pltpu.roll(s, d, 0), 0.0)
            sw = pltpu.roll(sh, S5_STATE, 1)
            s = s + sh * pa_ref[g, k:k + 1, :] + sw * pb_ref[g, k:k + 1, :]
        s_prev = jnp.where(rows >= 1, pltpu.roll(s, 1, 0), 0.0)
        y = _dot(ub, wtoep_ref[g]) + _dot(s_prev.astype(BF16), wcar_ref[g]) + d_ref[g] * u
        o_ref[0, g] = jax.nn.gelu(y).astype(o_ref.dtype)


def _s5_core(u_chunks, tables):
    bsz, n_groups, n_chunks, row = u_chunks.shape
    w_inc, w_toep, w_car, apow_a, apow_b, dsk = tables
    n_steps = apow_a.shape[1]
    gb = S5_GROUP_BLOCK
    blk = lambda b, i: (b, i, 0, 0)
    wblk = lambda b, i: (i, 0, 0)
    return pl.pallas_call(
        functools.partial(_s5_kernel, n_steps=n_steps),
        grid=(bsz, n_groups // gb),
        in_specs=[pl.BlockSpec((1, gb, n_chunks, row), blk),
                  pl.BlockSpec((gb,) + w_inc.shape[1:], wblk),
                  pl.BlockSpec((gb,) + w_toep.shape[1:], wblk),
                  pl.BlockSpec((gb,) + w_car.shape[1:], wblk),
                  pl.BlockSpec((gb,) + apow_a.shape[1:], wblk),
                  pl.BlockSpec((gb,) + apow_b.shape[1:], wblk),
                  pl.BlockSpec((gb,) + dsk.shape[1:], wblk)],
        out_specs=pl.BlockSpec((1, gb, n_chunks, row), blk),
        out_shape=jax.ShapeDtypeStruct(u_chunks.shape, BF16),
        compiler_params=_params("parallel", "arbitrary"),
        name="s5_scan",
    )(u_chunks, w_inc, w_toep, w_car, apow_a, apow_b, dsk)


def _kv_proj_kernel(x_ref, w_ref, o_ref):
    acc = _dot(x_ref[0], w_ref[...])
    for s in range(o_ref.shape[1]):
        o_ref[0, s] = acc[:, s * NSA_HEAD_DIM:(s + 1) * NSA_HEAD_DIM].astype(o_ref.dtype)


def _kv_proj(xb, w, out_dtype, *, tm=1024, slabs=8):
    bsz, seq, d = xb.shape
    n_slabs = w.shape[1] // NSA_HEAD_DIM
    tm = min(tm, seq)
    return pl.pallas_call(
        _kv_proj_kernel,
        grid=(bsz, seq // tm, n_slabs // slabs),
        in_specs=[pl.BlockSpec((1, tm, d), lambda b, i, j: (b, i, 0)),
                  pl.BlockSpec((d, slabs * NSA_HEAD_DIM), lambda b, i, j: (0, j))],
        out_specs=pl.BlockSpec((1, slabs, tm, NSA_HEAD_DIM), lambda b, i, j: (b, j, i, 0)),
        out_shape=jax.ShapeDtypeStruct((bsz, n_slabs, seq, NSA_HEAD_DIM), out_dtype),
        compiler_params=_params("parallel", "parallel", "arbitrary"),
        name="kv_proj",
    )(xb, w)


def _compress_kernel(t_ref, pos_ref, w1_ref, w2_ref, o_ref):
    t = t_ref[0, 0]
    n = t.shape[0]
    half = CMP_STRIDE * NSA_HEAD_DIM
    first = _dot((t + pos_ref[0, 0:1, :]).astype(BF16), w1_ref[0, :half, :])
    second = _dot((t + pos_ref[0, 1:2, :]).astype(BF16), w1_ref[0, half:, :])
    pre = first + pltpu.roll(second, n - 1, 0)
    out = _dot(jax.nn.gelu(pre).astype(BF16), w2_ref[0])
    rows = lax.broadcasted_iota(jnp.int32, out.shape, 0)
    o_ref[0, 0] = jnp.where(rows < n - 1, out, 0.0).astype(o_ref.dtype)


def _compress(kv_cmp, pos, w1, w2):
    bsz, n_slots, seq, hd = kv_cmp.shape
    n16 = seq // CMP_STRIDE
    t16 = kv_cmp.reshape(bsz, n_slots, n16, CMP_STRIDE * hd)
    kvh = NSA_KV_HEADS
    return pl.pallas_call(
        _compress_kernel,
        grid=(bsz, n_slots),
        in_specs=[pl.BlockSpec((1, 1, n16, CMP_STRIDE * hd), lambda b, s: (b, s, 0, 0)),
                  pl.BlockSpec((1, 2, CMP_STRIDE * hd), lambda b, s: (s // kvh, 0, 0)),
                  pl.BlockSpec((1, CMP_LEN * hd, hd), lambda b, s: (s // kvh, 0, 0)),
                  pl.BlockSpec((1, hd, hd), lambda b, s: (s // kvh, 0, 0))],
        out_specs=pl.BlockSpec((1, 1, n16, hd), lambda b, s: (b, s, 0, 0)),
        out_shape=jax.ShapeDtypeStruct((bsz, n_slots, n16, hd), BF16),
        compiler_params=_params("parallel", "arbitrary"),
        name="nsa_compress",
    )(t16, pos, w1, w2)


def _masked_softmax(s, mask):
    s = jnp.where(mask, s, NEG_BIG)
    m = jnp.max(s, axis=-1, keepdims=True)
    p = jnp.where(mask, jnp.exp(s - m), 0.0)
    return p / jnp.maximum(jnp.sum(p, axis=-1, keepdims=True), 1e-30)


def _nsa_kernel(q_ref, g_ref, kc_ref, vc_ref, ks_ref, vs_ref, kw_ref, vw_ref, o_ref):
    qb = pl.program_id(2)
    s0 = qb * Q_BLOCK
    R, hd, QB = NSA_REP, NSA_HEAD_DIM, Q_BLOCK
    rows = R * QB
    n_cmp = kc_ref.shape[2]
    n_sel = ks_ref.shape[2] // SEL_LEN
    seq = ks_ref.shape[2]

    q4 = jnp.concatenate([q_ref[0, :, r * hd:(r + 1) * hd] for r in range(R)], axis=0)
    t_rows = s0 + lax.broadcasted_iota(jnp.int32, (rows, 1), 0) % QB

    kc = kc_ref[0, 0]
    vc = vc_ref[0, 0]
    cmp_end = lax.broadcasted_iota(jnp.int32, (1, n_cmp), 1) * CMP_STRIDE + (CMP_LEN - 1)
    pc = _masked_softmax(_dot_nt(q4, kc), cmp_end <= t_rows)
    o_cmp = _dot(pc.astype(BF16), vc)

    pc_sum = pc[0:QB]
    for r in range(1, R):
        pc_sum = pc_sum + pc[r * QB:(r + 1) * QB]
    pc_hi = pc_sum.astype(BF16)
    pc_lo = (pc_sum - pc_hi.astype(F32)).astype(BF16)
    sel_s = lax.broadcasted_iota(jnp.int32, (n_sel, n_cmp), 0) * SEL_LEN
    cmp_c = lax.broadcasted_iota(jnp.int32, (n_sel, n_cmp), 1) * CMP_STRIDE
    sel_map_t = jnp.where((cmp_c < sel_s + SEL_LEN) & (cmp_c + CMP_LEN > sel_s), 1.0, 0.0).astype(BF16)
    imp = _dot_nt(sel_map_t, pc_hi) + _dot_nt(sel_map_t, pc_lo)

    sid = lax.broadcasted_iota(jnp.int32, (n_sel, QB), 0)
    sid_f = sid.astype(F32)
    tq = s0 + lax.broadcasted_iota(jnp.int32, (n_sel, QB), 1)
    cur = tq // SEL_LEN
    forced = (sid == 0) | (sid == cur) | (sid == cur - 1)
    imp = jnp.where(forced, SEL_FORCE, jnp.where(sid * SEL_LEN <= tq, imp, -SEL_FORCE))

    top_n = min(SEL_TOPN, n_sel)
    sel = jnp.zeros((n_sel, QB), F32)
    for _ in range(top_n):
        best = jnp.max(imp, axis=0, keepdims=True)
        first = jnp.min(jnp.where(imp == best, sid_f, float(n_sel)), axis=0, keepdims=True)
        hit = sid_f == first
        sel = jnp.where(hit, 1.0, sel)
        imp = jnp.where(hit, -jnp.inf, imp)
    bias = jnp.where(sel.T > 0.5, 0.0, NEG_BIG).astype(BF16)
    q_aug = jnp.concatenate([q4, jnp.concatenate([bias] * R, axis=0)], axis=1)

    key_lane = lax.broadcasted_iota(jnp.int32, (SEL_TILE, n_sel), 1)
    key_blk = lax.broadcasted_iota(jnp.int32, (SEL_TILE, n_sel), 0) // SEL_LEN
    kpos0 = lax.broadcasted_iota(jnp.int32, (1, SEL_TILE), 1)

    def sel_step(j, carry):
        m_prev, l_prev, acc = carry
        base = pl.multiple_of(j * SEL_TILE, SEL_TILE)
        kt = ks_ref[0, 0, pl.ds(base, SEL_TILE), :]
        vt = vs_ref[0, 0, pl.ds(base, SEL_TILE), :]
        onehot = jnp.where(key_lane == key_blk + j * (SEL_TILE // SEL_LEN), 1.0, 0.0).astype(BF16)
        s = _dot_nt(q_aug, jnp.concatenate([kt, onehot], axis=1))
        s = jnp.where(kpos0 + base <= t_rows, s, NEG_BIG)
        m_new = jnp.maximum(m_prev, jnp.max(s, axis=-1, keepdims=True))
        corr = jnp.exp(m_prev - m_new)
        p = jnp.exp(s - m_new)
        l_new = l_prev * corr + jnp.sum(p, axis=-1, keepdims=True)
        acc = acc * corr + _dot(p.astype(BF16), vt)
        return m_new, l_new, acc

    n_tiles = (s0 + QB - 1) // SEL_TILE + 1
    init = (jnp.full((rows, 1), NEG_BIG, F32), jnp.zeros((rows, 1), F32), jnp.zeros((rows, hd), F32))
    _, l_sel, acc_sel = lax.fori_loop(0, n_tiles, sel_step, init)
    o_slc = acc_sel / l_sel

    n_win = min(WINDOW + QB, seq)
    start = pl.multiple_of(jnp.maximum(s0 - WINDOW, 0), QB)
    kw = kw_ref[0, 0, pl.ds(start, n_win), :]
    vw = vw_ref[0, 0, pl.ds(start, n_win), :]
    kp = start + lax.broadcasted_iota(jnp.int32, (1, n_win), 1)
    pw = _masked_softmax(_dot_nt(q4, kw), (kp <= t_rows) & (kp > t_rows - WINDOW))
    o_win = _dot(pw.astype(BF16), vw)

    gates = g_ref[0]
    for r in range(R):
        sl = slice(r * QB, (r + 1) * QB)
        c = r * N_GATES
        out = (gates[:, c:c + 1] * o_cmp[sl] + gates[:, c + 1:c + 2] * o_slc[sl]
               + gates[:, c + 2:c + 3] * o_win[sl])
        o_ref[0, :, r * hd:(r + 1) * hd] = out.astype(o_ref.dtype)


def _nsa_attention(q, gates, kv_cmp, kv_rest):
    bsz, seq, _ = q.shape
    G, hd = NSA_KV_HEADS, NSA_HEAD_DIM
    n_cmp = kv_cmp.shape[2]
    gw = NSA_REP * hd

    def slot(k):
        return lambda b, g, i: (b, k * G + g, 0, 0)

    return pl.pallas_call(
        _nsa_kernel,
        grid=(bsz, G, seq // Q_BLOCK),
        in_specs=[pl.BlockSpec((1, Q_BLOCK, gw), lambda b, g, i: (b, i, g)),
                  pl.BlockSpec((1, Q_BLOCK, LANES), lambda b, g, i: (b, i, g)),
                  pl.BlockSpec((1, 1, n_cmp, hd), slot(0)),
                  pl.BlockSpec((1, 1, n_cmp, hd), slot(1)),
                  pl.BlockSpec((1, 1, seq, hd), slot(0)),
                  pl.BlockSpec((1, 1, seq, hd), slot(1)),
                  pl.BlockSpec((1, 1, seq, hd), slot(2)),
                  pl.BlockSpec((1, 1, seq, hd), slot(3))],
        out_specs=pl.BlockSpec((1, Q_BLOCK, gw), lambda b, g, i: (b, i, g)),
        out_shape=jax.ShapeDtypeStruct(q.shape, BF16),
        compiler_params=_params("parallel", "parallel", "arbitrary"),
        name="nsa_attention",
    )(q, gates, kv_cmp, kv_cmp, kv_rest, kv_rest, kv_rest, kv_rest)


def _s5_layer(x, xin, bsz, seq, w_in, tables, w_glu, w_out, g, b):
    m = bsz * seq
    n_chunks = seq // S5_CHUNK
    u = _mm(xin, w_in, F32, tm=1024, tn=1024)
    u = u.reshape(bsz, n_chunks, S5_CHUNK, S5_GROUPS, S5_GROUP).transpose(0, 3, 1, 2, 4)
    y = _s5_core(u.reshape(bsz, S5_GROUPS, n_chunks, S5_ROW), tables)
    y = y.reshape(bsz, S5_GROUPS, n_chunks, S5_CHUNK, S5_GROUP).transpose(0, 2, 3, 1, 4).reshape(m, D_MODEL)
    z = _mm_glu(y, w_glu)
    return _mm_res_ln(z, w_out, x, g, b)


def _nsa_layer(x, xb, bsz, seq, kv_cmp, kv_rest, w_q, w_gate, w_o, g, b):
    m = bsz * seq
    q = _mm(xb, w_q, BF16, tm=1024, tn=1024, scale=NSA_HEAD_DIM ** -0.5)
    gates = _mm(xb, w_gate, F32, tm=1024, tn=512, act="sigmoid")
    o = _nsa_attention(q.reshape(bsz, seq, -1), gates.reshape(bsz, seq, -1), kv_cmp, kv_rest)
    return _mm_res_ln(o.reshape(m, -1), w_o, x, g, b)


def _gate_weights(w_qg):
    n_q = NSA_HEADS * NSA_HEAD_DIM
    per = NSA_REP * N_GATES
    wg = w_qg[:, n_q:].reshape(-1, NSA_KV_HEADS, per)
    wg = jnp.pad(wg, ((0, 0), (0, 0), (0, LANES - per)))
    return wg.reshape(-1, NSA_KV_HEADS * LANES).astype(BF16)


def kernel(x, mem, s5_w_in, s5_a_re, s5_a_im, s5_log_dt, s5_b_re, s5_b_im, s5_c_re, s5_c_im, s5_d, s5_w_glu, s5_w_out, kv_w, cmp_pos_k, cmp_w1_k, cmp_w2_k, cmp_pos_v, cmp_w1_v, cmp_w2_v, nsa_w_qg, nsa_w_o, mem_w_q, mem_w_kv, mem_w_o, mlp_w_up, mlp_w_down, ln_g, ln_b):
    bsz, seq, d = x.shape
    m = bsz * seq
    bf = lambda w: w.astype(BF16)
    xf = x.reshape(m, d)
    xin = xf
    memf = mem.reshape(-1, d)
    kv_cmp = kv_rest = None
    for layer in range(DEPTH):
        if layer < N_A_LAYERS:
            i = layer
            tables = _s5_tables(s5_a_re[i], s5_a_im[i], s5_log_dt[i], s5_b_re[i], s5_b_im[i],
                                s5_c_re[i], s5_c_im[i], s5_d[i], seq // S5_CHUNK)
            xf, xin = _s5_layer(xf, xin, bsz, seq, bf(s5_w_in[i]), tables, bf(s5_w_glu[i]),
                                bf(s5_w_out[i]), ln_g[layer, 0], ln_b[layer, 0])
        else:
            i = layer - N_A_LAYERS
            if xin.dtype != BF16:
                xin = xin.astype(BF16)
            if i == 0:
                n_cmp_cols = 2 * NSA_KV_HEADS * NSA_HEAD_DIM
                x3 = xin.reshape(bsz, seq, d)
                kv_c = _kv_proj(x3, bf(kv_w[:, :n_cmp_cols]), F32)
                kv_rest = _kv_proj(x3, bf(kv_w[:, n_cmp_cols:]), BF16)
                half = CMP_STRIDE * NSA_HEAD_DIM
                pos = jnp.stack([cmp_pos_k.reshape(2, half), cmp_pos_v.reshape(2, half)])
                kv_cmp = _compress(kv_c, pos, jnp.stack([bf(cmp_w1_k), bf(cmp_w1_v)]),
                                   jnp.stack([bf(cmp_w2_k), bf(cmp_w2_v)]))
            n_q = NSA_HEADS * NSA_HEAD_DIM
            xf, xin = _nsa_layer(xf, xin, bsz, seq, kv_cmp, kv_rest, bf(nsa_w_qg[i][:, :n_q]),
                                 _gate_weights(nsa_w_qg[i]), bf(nsa_w_o[i]),
                                 ln_g[layer, 0], ln_b[layer, 0])
        kvm = _mm(memf, bf(mem_w_kv[layer]), BF16, tm=512, tn=1024)
        o = _mem_attn(xin.reshape(bsz, seq, d), bf(mem_w_q[layer]), kvm.reshape(bsz, -1, 2 * d))
        xf, xin = _mm_res_ln(o.reshape(m, d), bf(mem_w_o[layer]), xf, ln_g[layer, 1], ln_b[layer, 1])
        xf, xin = _mlp_ln(xin, xf, bf(mlp_w_up[layer]), bf(mlp_w_down[layer]),
                          ln_g[layer, 2], ln_b[layer, 2])
    return xf.reshape(bsz, seq, d)
```

```python
import functools
import math

import jax
import jax.numpy as jnp
import numpy as np
from jax import lax
from jax.experimental import pallas as pl
from jax.experimental.pallas import tpu as pltpu

F32 = jnp.float32
BF16 = jnp.bfloat16

D_MODEL = 2048
DEPTH = 2
N_A_LAYERS = DEPTH // 2

S5_GROUP = 16
S5_GROUPS = D_MODEL // S5_GROUP
S5_STATE = 64
S5_CHUNK = 16
S5_ROW = S5_CHUNK * S5_GROUP
S5_GROUP_BLOCK = 8

NSA_HEADS = 16
NSA_HEAD_DIM = D_MODEL // NSA_HEADS
NSA_KV_HEADS = 4
NSA_REP = NSA_HEADS // NSA_KV_HEADS
CMP_LEN = 32
CMP_STRIDE = 16
SEL_LEN = 64
SEL_TOPN = 16
WINDOW = 512
Q_BLOCK = 128
N_GATES = 3
N_KV_SLOTS = 6
SEL_TILE = 1024

MEM_HEADS = 4
MEM_HEAD_DIM = D_MODEL // MEM_HEADS

D_FF = 4 * D_MODEL

DN_ALPHA = float((2 * DEPTH) ** 0.25)
LN_EPS = 1e-5
NEG_BIG = -1e30
SEL_FORCE = 1e9

LANES = 128
VMEM_LIMIT = 56 * 1024 * 1024


def _params(*semantics):
    return pltpu.CompilerParams(dimension_semantics=semantics, vmem_limit_bytes=VMEM_LIMIT)


def _layer_norm(y, g, b):
    mu = jnp.mean(y, axis=-1, keepdims=True)
    yc = y - mu
    var = jnp.mean(yc * yc, axis=-1, keepdims=True)
    return yc * lax.rsqrt(var + LN_EPS) * g + b


def _dot(a, b):
    return jnp.dot(a, b, preferred_element_type=F32)


def _dot_nt(a, b):
    return lax.dot_general(a, b, (((1,), (1,)), ((), ())), preferred_element_type=F32)


def _mm_kernel(x_ref, w_ref, o_ref, *, scale, act):
    acc = _dot(x_ref[...].astype(BF16), w_ref[...])
    if scale is not None:
        acc = acc * scale
    if act == "sigmoid":
        acc = jax.nn.sigmoid(acc)
    o_ref[...] = acc.astype(o_ref.dtype)


def _mm(x, w, out_dtype, *, tm, tn, scale=None, act=None):
    m, k = x.shape
    n = w.shape[1]
    tm, tn = min(tm, m), min(tn, n)
    return pl.pallas_call(
        functools.partial(_mm_kernel, scale=scale, act=act),
        grid=(m // tm, n // tn),
        in_specs=[pl.BlockSpec((tm, k), lambda i, j: (i, 0)),
                  pl.BlockSpec((k, tn), lambda i, j: (0, j))],
        out_specs=pl.BlockSpec((tm, tn), lambda i, j: (i, j)),
        out_shape=jax.ShapeDtypeStruct((m, n), out_dtype),
        compiler_params=_params("parallel", "arbitrary"),
        name="mm",
    )(x, w)


def _glu_kernel(y_ref, wv_ref, wg_ref, o_ref):
    y = y_ref[...].astype(BF16)
    val = _dot(y, wv_ref[...])
    gate = _dot(y, wg_ref[...])
    o_ref[...] = (val * jax.nn.sigmoid(gate)).astype(o_ref.dtype)


def _mm_glu(y, w_glu, *, tm=1024, tn=512):
    m, k = y.shape
    n = w_glu.shape[1] // 2
    tm = min(tm, m)
    nb = n // tn
    return pl.pallas_call(
        _glu_kernel,
        grid=(m // tm, nb),
        in_specs=[pl.BlockSpec((tm, k), lambda i, j: (i, 0)),
                  pl.BlockSpec((k, tn), lambda i, j: (0, j)),
                  pl.BlockSpec((k, tn), lambda i, j: (0, j + nb))],
        out_specs=pl.BlockSpec((tm, tn), lambda i, j: (i, j)),
        out_shape=jax.ShapeDtypeStruct((m, n), BF16),
        compiler_params=_params("parallel", "arbitrary"),
        name="mm_glu",
    )(y, w_glu, w_glu)


def _res_ln_kernel(z_ref, w_ref, x_ref, g_ref, b_ref, o_ref, ob_ref):
    h = _dot(z_ref[...], w_ref[...])
    y = _layer_norm(DN_ALPHA * x_ref[...] + h, g_ref[...], b_ref[...])
    o_ref[...] = y
    ob_ref[...] = y.astype(BF16)


def _mm_res_ln(z, w, x, g, b, *, tm=512):
    m, k = z.shape
    d = w.shape[1]
    tm = min(tm, m)
    row = lambda i: (i, 0)
    fixed = lambda i: (0, 0)
    return pl.pallas_call(
        _res_ln_kernel,
        grid=(m // tm,),
        in_specs=[pl.BlockSpec((tm, k), row), pl.BlockSpec((k, d), fixed),
                  pl.BlockSpec((tm, d), row), pl.BlockSpec((1, d), fixed),
                  pl.BlockSpec((1, d), fixed)],
        out_specs=[pl.BlockSpec((tm, d), row), pl.BlockSpec((tm, d), row)],
        out_shape=[jax.ShapeDtypeStruct((m, d), F32), jax.ShapeDtypeStruct((m, d), BF16)],
        compiler_params=_params("parallel"),
        name="proj_postnorm",
    )(z, w, x, g.reshape(1, d), b.reshape(1, d))


def _mlp_kernel(xb_ref, x_ref, wu_ref, wd_ref, g_ref, b_ref, o_ref, ob_ref, acc_ref):
    j = pl.program_id(1)
    h = jnp.maximum(_dot(xb_ref[...], wu_ref[...]), 0.0)
    part = _dot((h * h).astype(BF16), wd_ref[...])

    @pl.when(j == 0)
    def _():
        acc_ref[...] = part

    @pl.when(j > 0)
    def _():
        acc_ref[...] += part

    @pl.when(j == pl.num_programs(1) - 1)
    def _():
        y = _layer_norm(DN_ALPHA * x_ref[...] + acc_ref[...], g_ref[...], b_ref[...])
        o_ref[...] = y
        ob_ref[...] = y.astype(BF16)


def _mlp_ln(xb, x, w_up, w_down, g, b, *, tm=512, tf=1024):
    m, d = x.shape
    ff = w_up.shape[1]
    tm = min(tm, m)
    row = lambda i, j: (i, 0)
    fixed = lambda i, j: (0, 0)
    return pl.pallas_call(
        _mlp_kernel,
        grid=(m // tm, ff // tf),
        in_specs=[pl.BlockSpec((tm, d), row), pl.BlockSpec((tm, d), row),
                  pl.BlockSpec((d, tf), lambda i, j: (0, j)),
                  pl.BlockSpec((tf, d), lambda i, j: (j, 0)),
                  pl.BlockSpec((1, d), fixed), pl.BlockSpec((1, d), fixed)],
        out_specs=[pl.BlockSpec((tm, d), row), pl.BlockSpec((tm, d), row)],
        out_shape=[jax.ShapeDtypeStruct((m, d), F32), jax.ShapeDtypeStruct((m, d), BF16)],
        scratch_shapes=[pltpu.VMEM((tm, d), F32)],
        compiler_params=_params("parallel", "arbitrary"),
        name="mlp_postnorm",
    )(xb, x, w_up, w_down, g.reshape(1, d), b.reshape(1, d))


def _mem_attn_kernel(x_ref, wq_ref, kv_ref, o_ref):
    q = _dot(x_ref[0], wq_ref[...]).astype(BF16)
    scale = MEM_HEAD_DIM ** -0.5
    for h in range(MEM_HEADS):
        lo = h * MEM_HEAD_DIM
        k = kv_ref[0, :, lo:lo + MEM_HEAD_DIM]
        v = kv_ref[0, :, D_MODEL + lo:D_MODEL + lo + MEM_HEAD_DIM]
        s = _dot_nt(q[:, lo:lo + MEM_HEAD_DIM], k) * scale
        p = jnp.exp(s - jnp.max(s, axis=-1, keepdims=True))
        p = p / jnp.sum(p, axis=-1, keepdims=True)
        o_ref[0, :, lo:lo + MEM_HEAD_DIM] = _dot(p.astype(BF16), v).astype(o_ref.dtype)


def _mem_attn(xb, wq, kvm, *, tm=512):
    bsz, seq, d = xb.shape
    n_mem = kvm.shape[1]
    tm = min(tm, seq)
    return pl.pallas_call(
        _mem_attn_kernel,
        grid=(bsz, seq // tm),
        in_specs=[pl.BlockSpec((1, tm, d), lambda b, i: (b, i, 0)),
                  pl.BlockSpec((d, d), lambda b, i: (0, 0)),
                  pl.BlockSpec((1, n_mem, 2 * d), lambda b, i: (b, 0, 0))],
        out_specs=pl.BlockSpec((1, tm, d), lambda b, i: (b, i, 0)),
        out_shape=jax.ShapeDtypeStruct((bsz, seq, d), BF16),
        compiler_params=_params("parallel", "arbitrary"),
        name="mem_attn",
    )(xb, wq, kvm)


def _s5_tables(a_re, a_im, log_dt, b_re, b_im, c_re, c_im, n_chunks):
    hi = lax.Precision.HIGHEST
    G, P, GS, T = S5_GROUPS, S5_STATE, S5_GROUP, S5_CHUNK
    dt = jnp.exp(log_dt.astype(F32))[:, None]
    lr, li = a_re.astype(F32), a_im.astype(F32)
    mag = jnp.exp(lr * dt)
    ab_re, ab_im = mag * jnp.cos(li * dt), mag * jnp.sin(li * dt)
    den = lr * lr + li * li
    nr, ni = ab_re - 1.0, ab_im
    f_re = (nr * lr + ni * li) / den
    f_im = (ni * lr - nr * li) / den
    br, bi = b_re.astype(F32), b_im.astype(F32)
    bb_re = f_re[..., None] * br - f_im[..., None] * bi
    bb_im = f_re[..., None] * bi + f_im[..., None] * br

    def apow(n):
        n = jnp.asarray(n, F32)[:, None, None]
        m = jnp.exp(lr * dt * n)
        return m * jnp.cos(li * dt * n), m * jnp.sin(li * dt * n)

    pw_re, pw_im = apow(np.arange(T + 1))
    abb_re = pw_re[..., None] * bb_re - pw_im[..., None] * bb_im
    abb_im = pw_re[..., None] * bb_im + pw_im[..., None] * bb_re

    rev = np.arange(T - 1, -1, -1)
    w_inc = jnp.concatenate([abb_re[rev], abb_im[rev]], axis=2)
    w_inc = w_inc.transpose(1, 0, 3, 2).reshape(G, T * GS, 2 * P)

    cr, ci = c_re.astype(F32), c_im.astype(F32)
    kern = (jnp.einsum("ghp,ngpk->nghk", cr, abb_re[:T], precision=hi)
            - jnp.einsum("ghp,ngpk->nghk", ci, abb_im[:T], precision=hi))
    tau = np.arange(T)[None, :] - np.arange(T)[:, None]
    toep = jnp.where((tau >= 0)[:, :, None, None, None], kern[np.clip(tau, 0, T - 1)], 0.0)
    w_toep = toep.transpose(2, 0, 4, 1, 3).reshape(G, T * GS, T * GS)

    pr, pi = pw_re[1:], pw_im[1:]
    car_re = cr[None] * pr[:, :, None, :] - ci[None] * pi[:, :, None, :]
    car_im = cr[None] * pi[:, :, None, :] + ci[None] * pr[:, :, None, :]
    w_car = jnp.concatenate([car_re, -car_im], axis=3)
    w_car = w_car.transpose(1, 3, 0, 2).reshape(G, 2 * P, T * GS)

    n_steps = max(1, int(math.ceil(math.log2(n_chunks))))
    sr, si = apow(T * (2 ** np.arange(n_steps)))
    apow_a = jnp.concatenate([sr, sr], axis=2).transpose(1, 0, 2)
    apow_b = jnp.concatenate([-si, si], axis=2).transpose(1, 0, 2)
    return w_inc.astype(BF16), w_toep.astype(BF16), w_car.astype(BF16), apow_a, apow_b


def _s5_kernel(u_ref, winc_ref, wtoep_ref, wcar_ref, pa_ref, pb_ref, d_ref, o_ref, *, n_steps):
    T, GS, GB = S5_CHUNK, S5_GROUP, S5_GROUP_BLOCK
    n_chunks = u_ref.shape[1] // T
    rows = lax.broadcasted_iota(jnp.int32, (n_chunks, 2 * S5_STATE), 0)
    xs = [u_ref[0, pl.ds(j, n_chunks, stride=T), :] for j in range(T)]
    ys = []
    for g in range(GB):
        ub = jnp.concatenate([x[:, g * GS:(g + 1) * GS] for x in xs], axis=1).astype(BF16)
        s = _dot(ub, winc_ref[g])
        for k in range(n_steps):
            d = 1 << k
            sh = jnp.where(rows >= d, pltpu.roll(s, d, 0), 0.0)
            sw = pltpu.roll(sh, S5_STATE, 1)
            s = s + sh * pa_ref[g, k:k + 1, :] + sw * pb_ref[g, k:k + 1, :]
        s_prev = jnp.where(rows >= 1, pltpu.roll(s, 1, 0), 0.0)
        ys.append(_dot(ub, wtoep_ref[g]) + _dot(s_prev.astype(BF16), wcar_ref[g]))
    for t in range(T):
        y = jnp.concatenate([yg[:, t * GS:(t + 1) * GS] for yg in ys], axis=1)
        y = y + d_ref[...] * xs[t]
        o_ref[0, pl.ds(t, n_chunks, stride=T), :] = jax.nn.gelu(y)


def _s5_core(u, tables, d_skip):
    bsz, seq, d = u.shape
    w_inc, w_toep, w_car, apow_a, apow_b = tables
    n_steps = apow_a.shape[1]
    gb = S5_GROUP_BLOCK
    blk = lambda b, i: (b, 0, i)
    wblk = lambda b, i: (i, 0, 0)
    return pl.pallas_call(
        functools.partial(_s5_kernel, n_steps=n_steps),
        grid=(bsz, S5_GROUPS // gb),
        in_specs=[pl.BlockSpec((1, seq, LANES), blk),
                  pl.BlockSpec((gb,) + w_inc.shape[1:], wblk),
                  pl.BlockSpec((gb,) + w_toep.shape[1:], wblk),
                  pl.BlockSpec((gb,) + w_car.shape[1:], wblk),
                  pl.BlockSpec((gb,) + apow_a.shape[1:], wblk),
                  pl.BlockSpec((gb,) + apow_b.shape[1:], wblk),
                  pl.BlockSpec((1, LANES), lambda b, i: (0, i))],
        out_specs=pl.BlockSpec((1, seq, LANES), blk),
        out_shape=jax.ShapeDtypeStruct(u.shape, F32),
        compiler_params=_params("parallel", "arbitrary"),
        name="s5_scan",
    )(u, w_inc, w_toep, w_car, apow_a, apow_b, d_skip.astype(F32).reshape(1, d))


def _kv_proj_kernel(x_ref, w_ref, o_ref):
    acc = _dot(x_ref[0], w_ref[...])
    for s in range(o_ref.shape[1]):
        o_ref[0, s] = acc[:, s * NSA_HEAD_DIM:(s + 1) * NSA_HEAD_DIM].astype(o_ref.dtype)


def _kv_proj(xb, w, out_dtype, *, tm=1024, slabs=8):
    bsz, seq, d = xb.shape
    n_slabs = w.shape[1] // NSA_HEAD_DIM
    tm = min(tm, seq)
    return pl.pallas_call(
        _kv_proj_kernel,
        grid=(bsz, seq // tm, n_slabs // slabs),
        in_specs=[pl.BlockSpec((1, tm, d), lambda b, i, j: (b, i, 0)),
                  pl.BlockSpec((d, slabs * NSA_HEAD_DIM), lambda b, i, j: (0, j))],
        out_specs=pl.BlockSpec((1, slabs, tm, NSA_HEAD_DIM), lambda b, i, j: (b, j, i, 0)),
        out_shape=jax.ShapeDtypeStruct((bsz, n_slabs, seq, NSA_HEAD_DIM), out_dtype),
        compiler_params=_params("parallel", "parallel", "arbitrary"),
        name="kv_proj",
    )(xb, w)


def _compress_kernel(t_ref, pos_ref, w1_ref, w2_ref, o_ref):
    t = t_ref[0, 0]
    n = t.shape[0]
    half = CMP_STRIDE * NSA_HEAD_DIM
    first = _dot((t + pos_ref[0, 0:1, :]).astype(BF16), w1_ref[0, :half, :])
    second = _dot((t + pos_ref[0, 1:2, :]).astype(BF16), w1_ref[0, half:, :])
    pre = first + pltpu.roll(second, n - 1, 0)
    out = _dot(jax.nn.gelu(pre).astype(BF16), w2_ref[0])
    rows = lax.broadcasted_iota(jnp.int32, out.shape, 0)
    o_ref[0, 0] = jnp.where(rows < n - 1, out, 0.0).astype(o_ref.dtype)


def _compress(kv_cmp, pos, w1, w2):
    bsz, n_slots, seq, hd = kv_cmp.shape
    n16 = seq // CMP_STRIDE
    t16 = kv_cmp.reshape(bsz, n_slots, n16, CMP_STRIDE * hd)
    kvh = NSA_KV_HEADS
    return pl.pallas_call(
        _compress_kernel,
        grid=(bsz, n_slots),
        in_specs=[pl.BlockSpec((1, 1, n16, CMP_STRIDE * hd), lambda b, s: (b, s, 0, 0)),
                  pl.BlockSpec((1, 2, CMP_STRIDE * hd), lambda b, s: (s // kvh, 0, 0)),
                  pl.BlockSpec((1, CMP_LEN * hd, hd), lambda b, s: (s // kvh, 0, 0)),
                  pl.BlockSpec((1, hd, hd), lambda b, s: (s // kvh, 0, 0))],
        out_specs=pl.BlockSpec((1, 1, n16, hd), lambda b, s: (b, s, 0, 0)),
        out_shape=jax.ShapeDtypeStruct((bsz, n_slots, n16, hd), BF16),
        compiler_params=_params("parallel", "arbitrary"),
        name="nsa_compress",
    )(t16, pos, w1, w2)


def _nsa_kernel(q_ref, g_ref, kc_ref, vc_ref, ks_ref, vs_ref, kw_ref, vw_ref, o_ref, kaug_ref, band_ref):
    qb = pl.program_id(2)
    s0 = qb * Q_BLOCK
    R, hd, QB = NSA_REP, NSA_HEAD_DIM, Q_BLOCK
    rows = R * QB
    n_cmp = kc_ref.shape[2]
    seq = ks_ref.shape[2]
    n_sel = seq // SEL_LEN
    n_win = band_ref.shape[1]

    @pl.when(qb == 0)
    def _():
        lane = lax.broadcasted_iota(jnp.int32, (SEL_TILE, n_sel), 1)
        blk = lax.broadcasted_iota(jnp.int32, (SEL_TILE, n_sel), 0) // SEL_LEN

        def fill(j, _):
            base = pl.multiple_of(j * SEL_TILE, SEL_TILE)
            onehot = jnp.where(lane == blk + j * (SEL_TILE // SEL_LEN), 1.0, 0.0).astype(BF16)
            kaug_ref[pl.ds(base, SEL_TILE), :] = jnp.concatenate(
                [ks_ref[0, 0, pl.ds(base, SEL_TILE), :], onehot], axis=1)
            return 0

        lax.fori_loop(0, seq // SEL_TILE, fill, 0)
        qi = lax.broadcasted_iota(jnp.int32, (QB, n_win), 0)
        col = lax.broadcasted_iota(jnp.int32, (QB, n_win), 1)
        band_ref[...] = jnp.where((col > qi) & (col <= qi + WINDOW), 0.0, NEG_BIG)

    q4 = jnp.concatenate([q_ref[0, :, r * hd:(r + 1) * hd] for r in range(R)], axis=0)
    t_rows = s0 + lax.broadcasted_iota(jnp.int32, (rows, 1), 0) % QB

    cmp_end = lax.broadcasted_iota(jnp.int32, (1, n_cmp), 1) * CMP_STRIDE + (CMP_LEN - 1)
    sc = jnp.where(cmp_end <= t_rows, _dot_nt(q4, kc_ref[0, 0]), NEG_BIG)
    mc = jnp.max(sc, axis=-1, keepdims=True)
    pc = jnp.exp(sc - mc)
    lc = jnp.sum(pc, axis=-1, keepdims=True)
    pc = pc * jnp.where(mc > 0.5 * NEG_BIG, 1.0 / jnp.maximum(lc, 1e-30), 0.0)
    o_cmp = _dot(pc.astype(BF16), vc_ref[0, 0])

    pc_sum = pc[0:QB]
    for r in range(1, R):
        pc_sum = pc_sum + pc[r * QB:(r + 1) * QB]
    pc_hi = pc_sum.astype(BF16)
    pc_lo = (pc_sum - pc_hi.astype(F32)).astype(BF16)
    sel_s = lax.broadcasted_iota(jnp.int32, (n_sel, n_cmp), 0) * SEL_LEN
    cmp_c = lax.broadcasted_iota(jnp.int32, (n_sel, n_cmp), 1) * CMP_STRIDE
    sel_map_t = jnp.where((cmp_c < sel_s + SEL_LEN) & (cmp_c + CMP_LEN > sel_s), 1.0, 0.0).astype(BF16)
    imp = _dot_nt(sel_map_t, pc_hi) + _dot_nt(sel_map_t, pc_lo)

    sid = lax.broadcasted_iota(jnp.int32, (n_sel, QB), 0)
    sid_f = sid.astype(F32)
    tq = s0 + lax.broadcasted_iota(jnp.int32, (n_sel, QB), 1)
    cur = tq // SEL_LEN
    forced = (sid == 0) | (sid == cur) | (sid == cur - 1)
    imp = jnp.where(forced, SEL_FORCE, jnp.where(sid * SEL_LEN <= tq, imp, -SEL_FORCE))

    top_n = min(SEL_TOPN, n_sel)
    sel = jnp.zeros((n_sel, QB), F32)
    for _ in range(top_n):
        best = jnp.max(imp, axis=0, keepdims=True)
        first = jnp.min(jnp.where(imp == best, sid_f, float(n_sel)), axis=0, keepdims=True)
        hit = sid_f == first
        sel = jnp.where(hit, 1.0, sel)
        imp = jnp.where(hit, -jnp.inf, imp)
    bias = jnp.where(sel.T > 0.5, 0.0, NEG_BIG).astype(BF16)
    q_aug = jnp.concatenate([q4, jnp.concatenate([bias] * R, axis=0)], axis=1)

    def sel_tile(j, carry, causal):
        m_prev, l_prev, acc = carry
        base = pl.multiple_of(j * SEL_TILE, SEL_TILE)
        s = _dot_nt(q_aug, kaug_ref[pl.ds(base, SEL_TILE), :])
        if causal:
            kpos = base + lax.broadcasted_iota(jnp.int32, (1, SEL_TILE), 1)
            s = jnp.where(kpos <= t_rows, s, NEG_BIG)
        m_new = jnp.maximum(m_prev, jnp.max(s, axis=-1, keepdims=True))
        corr = jnp.exp(m_prev - m_new)
        p = jnp.exp(s - m_new)
        l_new = l_prev * corr + jnp.sum(p, axis=-1, keepdims=True)
        acc = acc * corr + _dot(p.astype(BF16), vs_ref[0, 0, pl.ds(base, SEL_TILE), :])
        return m_new, l_new, acc

    n_past = s0 // SEL_TILE
    init = (jnp.full((rows, 1), NEG_BIG, F32), jnp.zeros((rows, 1), F32), jnp.zeros((rows, hd), F32))
    carry = lax.fori_loop(0, n_past, functools.partial(sel_tile, causal=False), init)
    _, l_sel, acc_sel = sel_tile(n_past, carry, True)
    o_slc = acc_sel * (1.0 / l_sel)

    start = pl.multiple_of(s0, QB)
    col = lax.broadcasted_iota(jnp.int32, (1, n_win), 1)
    win_bias = band_ref[...] + jnp.where(col + s0 >= WINDOW, 0.0, NEG_BIG)
    sw = _dot_nt(q4, kw_ref[0, 0, pl.ds(start, n_win), :])
    sw = jnp.concatenate([sw[r * QB:(r + 1) * QB] + win_bias for r in range(R)], axis=0)
    pw = jnp.exp(sw - jnp.max(sw, axis=-1, keepdims=True))
    lw = jnp.sum(pw, axis=-1, keepdims=True)
    o_win = _dot(pw.astype(BF16), vw_ref[0, 0, pl.ds(start, n_win), :]) * (1.0 / lw)

    gates = g_ref[0]
    for r in range(R):
        sl = slice(r * QB, (r + 1) * QB)
        c = r * N_GATES
        out = (gates[:, c:c + 1] * o_cmp[sl] + gates[:, c + 1:c + 2] * o_slc[sl]
               + gates[:, c + 2:c + 3] * o_win[sl])
        o_ref[0, :, r * hd:(r + 1) * hd] = out.astype(o_ref.dtype)


def _nsa_attention(q, gates, kv_cmp, kv_sel, kv_win):
    bsz, seq, _ = q.shape
    G, hd = NSA_KV_HEADS, NSA_HEAD_DIM
    n_cmp = kv_cmp.shape[2]
    gw = NSA_REP * hd
    n_win = WINDOW + Q_BLOCK

    def slot(k):
        return lambda b, g, i: (b, k * G + g, 0, 0)

    return pl.pallas_call(
        _nsa_kernel,
        grid=(bsz, G, seq // Q_BLOCK),
        in_specs=[pl.BlockSpec((1, Q_BLOCK, gw), lambda b, g, i: (b, i, g)),
                  pl.BlockSpec((1, Q_BLOCK, LANES), lambda b, g, i: (b, i, g)),
                  pl.BlockSpec((1, 1, n_cmp, hd), slot(0)),
                  pl.BlockSpec((1, 1, n_cmp, hd), slot(1)),
                  pl.BlockSpec((1, 1, seq, hd), slot(0)),
                  pl.BlockSpec((1, 1, seq, hd), slot(1)),
                  pl.BlockSpec((1, 1, WINDOW + seq, hd), slot(0)),
                  pl.BlockSpec((1, 1, WINDOW + seq, hd), slot(1))],
        out_specs=pl.BlockSpec((1, Q_BLOCK, gw), lambda b, g, i: (b, i, g)),
        out_shape=jax.ShapeDtypeStruct(q.shape, BF16),
        scratch_shapes=[pltpu.VMEM((seq, hd + seq // SEL_LEN), BF16),
                        pltpu.VMEM((Q_BLOCK, n_win), F32)],
        compiler_params=_params("arbitrary", "arbitrary", "arbitrary"),
        name="nsa_attention",
    )(q, gates, kv_cmp, kv_cmp, kv_sel, kv_sel, kv_win, kv_win)


def _s5_layer(x, xin, bsz, seq, w_in, tables, d_skip, w_glu, w_out, g, b):
    m = bsz * seq
    u = _mm(xin, w_in, F32, tm=1024, tn=1024)
    y = _s5_core(u.reshape(bsz, seq, D_MODEL), tables, d_skip)
    z = _mm_glu(y.reshape(m, D_MODEL), w_glu)
    return _mm_res_ln(z, w_out, x, g, b)


def _nsa_layer(x, xb, bsz, seq, kv_cmp, kv_sel, kv_win, w_q, w_gate, w_o, g, b):
    m = bsz * seq
    q = _mm(xb, w_q, BF16, tm=1024, tn=1024, scale=NSA_HEAD_DIM ** -0.5)
    gates = _mm(xb, w_gate, F32, tm=1024, tn=512, act="sigmoid")
    o = _nsa_attention(q.reshape(bsz, seq, -1), gates.reshape(bsz, seq, -1), kv_cmp, kv_sel, kv_win)
    return _mm_res_ln(o.reshape(m, -1), w_o, x, g, b)


def _gate_weights(w_qg):
    n_q = NSA_HEADS * NSA_HEAD_DIM
    per = NSA_REP * N_GATES
    wg = w_qg[:, n_q:].reshape(-1, NSA_KV_HEADS, per)
    wg = jnp.pad(wg, ((0, 0), (0, 0), (0, LANES - per)))
    return wg.reshape(-1, NSA_KV_HEADS * LANES).astype(BF16)


def kernel(x, mem, s5_w_in, s5_a_re, s5_a_im, s5_log_dt, s5_b_re, s5_b_im, s5_c_re, s5_c_im, s5_d, s5_w_glu, s5_w_out, kv_w, cmp_pos_k, cmp_w1_k, cmp_w2_k, cmp_pos_v, cmp_w1_v, cmp_w2_v, nsa_w_qg, nsa_w_o, mem_w_q, mem_w_kv, mem_w_o, mlp_w_up, mlp_w_down, ln_g, ln_b):
    bsz, seq, d = x.shape
    m = bsz * seq
    bf = lambda w: w.astype(BF16)
    xf = x.reshape(m, d)
    xin = xf
    memf = mem.reshape(-1, d)
    kv_cmp = kv_sel = kv_win = None
    for layer in range(DEPTH):
        if layer < N_A_LAYERS:
            i = layer
            tables = _s5_tables(s5_a_re[i], s5_a_im[i], s5_log_dt[i], s5_b_re[i], s5_b_im[i],
                                s5_c_re[i], s5_c_im[i], seq // S5_CHUNK)
            xf, xin = _s5_layer(xf, xin, bsz, seq, bf(s5_w_in[i]), tables, s5_d[i], bf(s5_w_glu[i]),
                                bf(s5_w_out[i]), ln_g[layer, 0], ln_b[layer, 0])
        else:
            i = layer - N_A_LAYERS
            if xin.dtype != BF16:
                xin = xin.astype(BF16)
            if i == 0:
                n_cmp_cols = 2 * NSA_KV_HEADS * NSA_HEAD_DIM
                x3 = xin.reshape(bsz, seq, d)
                kv_c = _kv_proj(x3, bf(kv_w[:, :n_cmp_cols]), F32)
                kv_sel = _kv_proj(x3, bf(kv_w[:, n_cmp_cols:]), BF16)
                kv_win = jnp.pad(kv_sel[:, 2 * NSA_KV_HEADS:], ((0, 0), (0, 0), (WINDOW, 0), (0, 0)))
                half = CMP_STRIDE * NSA_HEAD_DIM
                pos = jnp.stack([cmp_pos_k.reshape(2, half), cmp_pos_v.reshape(2, half)])
                kv_cmp = _compress(kv_c, pos, jnp.stack([bf(cmp_w1_k), bf(cmp_w1_v)]),
                                   jnp.stack([bf(cmp_w2_k), bf(cmp_w2_v)]))
            n_q = NSA_HEADS * NSA_HEAD_DIM
            xf, xin = _nsa_layer(xf, xin, bsz, seq, kv_cmp, kv_sel, kv_win, bf(nsa_w_qg[i][:, :n_q]),
                                 _gate_weights(nsa_w_qg[i]), bf(nsa_w_o[i]),
                                 ln_g[layer, 0], ln_b[layer, 0])
        kvm = _mm(memf, bf(mem_w_kv[layer]), BF16, tm=512, tn=1024)
        o = _mem_attn(xin.reshape(bsz, seq, d), bf(mem_w_q[layer]), kvm.reshape(bsz, -1, 2 * d))
        xf, xin = _mm_res_ln(o.reshape(m, d), bf(mem_w_o[layer]), xf, ln_g[layer, 1], ln_b[layer, 1])
        xf, xin = _mlp_ln(xin, xf, bf(mlp_w_up[layer]), bf(mlp_w_down[layer]),
                          ln_g[layer, 2], ln_b[layer, 2])
    return xf.reshape(bsz, seq, d)
```

```python
import functools
import math

import jax
import jax.numpy as jnp
import numpy as np
from jax import lax
from jax.experimental import pallas as pl
from jax.experimental.pallas import tpu as pltpu

F32 = jnp.float32
BF16 = jnp.bfloat16

D_MODEL = 2048
DEPTH = 2
N_A_LAYERS = DEPTH // 2

S5_GROUP = 16
S5_GROUPS = D_MODEL // S5_GROUP
S5_STATE = 64
S5_CHUNK = 16
S5_ROW = S5_CHUNK * S5_GROUP
S5_GROUP_BLOCK = 8

NSA_HEADS = 16
NSA_HEAD_DIM = D_MODEL // NSA_HEADS
NSA_KV_HEADS = 4
NSA_REP = NSA_HEADS // NSA_KV_HEADS
CMP_LEN = 32
CMP_STRIDE = 16
SEL_LEN = 64
SEL_TOPN = 16
WINDOW = 512
Q_BLOCK = 128
N_GATES = 3
N_KV_SLOTS = 6
SEL_TILE = 1024

MEM_HEADS = 4
MEM_HEAD_DIM = D_MODEL // MEM_HEADS

D_FF = 4 * D_MODEL

DN_ALPHA = float((2 * DEPTH) ** 0.25)
LN_EPS = 1e-5
NEG_BIG = -1e30
SEL_FORCE = 1e9

LANES = 128
VMEM_LIMIT = 56 * 1024 * 1024


def _params(*semantics):
    return pltpu.CompilerParams(dimension_semantics=semantics, vmem_limit_bytes=VMEM_LIMIT)


def _layer_norm(y, g, b):
    mu = jnp.mean(y, axis=-1, keepdims=True)
    yc = y - mu
    var = jnp.mean(yc * yc, axis=-1, keepdims=True)
    return yc * lax.rsqrt(var + LN_EPS) * g + b


def _dot(a, b):
    return jnp.dot(a, b, preferred_element_type=F32)


def _dot_nt(a, b):
    return lax.dot_general(a, b, (((1,), (1,)), ((), ())), preferred_element_type=F32)


def _mm_kernel(x_ref, w_ref, o_ref, *, scale, act):
    acc = _dot(x_ref[...].astype(BF16), w_ref[...])
    if scale is not None:
        acc = acc * scale
    if act == "sigmoid":
        acc = jax.nn.sigmoid(acc)
    o_ref[...] = acc.astype(o_ref.dtype)


def _mm(x, w, out_dtype, *, tm, tn, scale=None, act=None):
    m, k = x.shape
    n = w.shape[1]
    tm, tn = min(tm, m), min(tn, n)
    return pl.pallas_call(
        functools.partial(_mm_kernel, scale=scale, act=act),
        grid=(m // tm, n // tn),
        in_specs=[pl.BlockSpec((tm, k), lambda i, j: (i, 0)),
                  pl.BlockSpec((k, tn), lambda i, j: (0, j))],
        out_specs=pl.BlockSpec((tm, tn), lambda i, j: (i, j)),
        out_shape=jax.ShapeDtypeStruct((m, n), out_dtype),
        compiler_params=_params("parallel", "arbitrary"),
        name="mm",
    )(x, w)


def _glu_kernel(y_ref, wv_ref, wg_ref, o_ref):
    y = y_ref[...].astype(BF16)
    val = _dot(y, wv_ref[...])
    gate = _dot(y, wg_ref[...])
    o_ref[...] = (val * jax.nn.sigmoid(gate)).astype(o_ref.dtype)


def _mm_glu(y, w_glu, *, tm=1024, tn=512):
    m, k = y.shape
    n = w_glu.shape[1] // 2
    tm = min(tm, m)
    nb = n // tn
    return pl.pallas_call(
        _glu_kernel,
        grid=(m // tm, nb),
        in_specs=[pl.BlockSpec((tm, k), lambda i, j: (i, 0)),
                  pl.BlockSpec((k, tn), lambda i, j: (0, j)),
                  pl.BlockSpec((k, tn), lambda i, j: (0, j + nb))],
        out_specs=pl.BlockSpec((tm, tn), lambda i, j: (i, j)),
        out_shape=jax.ShapeDtypeStruct((m, n), BF16),
        compiler_params=_params("parallel", "arbitrary"),
        name="mm_glu",
    )(y, w_glu, w_glu)


def _res_ln_kernel(z_ref, w_ref, x_ref, g_ref, b_ref, o_ref, ob_ref):
    h = _dot(z_ref[...], w_ref[...])
    y = _layer_norm(DN_ALPHA * x_ref[...] + h, g_ref[...], b_ref[...])
    o_ref[...] = y
    ob_ref[...] = y.astype(BF16)


def _mm_res_ln(z, w, x, g, b, *, tm=512):
    m, k = z.shape
    d = w.shape[1]
    tm = min(tm, m)
    row = lambda i: (i, 0)
    fixed = lambda i: (0, 0)
    return pl.pallas_call(
        _res_ln_kernel,
        grid=(m // tm,),
        in_specs=[pl.BlockSpec((tm, k), row), pl.BlockSpec((k, d), fixed),
                  pl.BlockSpec((tm, d), row), pl.BlockSpec((1, d), fixed),
                  pl.BlockSpec((1, d), fixed)],
        out_specs=[pl.BlockSpec((tm, d), row), pl.BlockSpec((tm, d), row)],
        out_shape=[jax.ShapeDtypeStruct((m, d), F32), jax.ShapeDtypeStruct((m, d), BF16)],
        compiler_params=_params("parallel"),
        name="proj_postnorm",
    )(z, w, x, g.reshape(1, d), b.reshape(1, d))


def _mlp_kernel(xb_ref, x_ref, wu_ref, wd_ref, g_ref, b_ref, o_ref, ob_ref, acc_ref):
    j = pl.program_id(1)
    h = jnp.maximum(_dot(xb_ref[...], wu_ref[...]), 0.0)
    part = _dot((h * h).astype(BF16), wd_ref[...])

    @pl.when(j == 0)
    def _():
        acc_ref[...] = part

    @pl.when(j > 0)
    def _():
        acc_ref[...] += part

    @pl.when(j == pl.num_programs(1) - 1)
    def _():
        y = _layer_norm(DN_ALPHA * x_ref[...] + acc_ref[...], g_ref[...], b_ref[...])
        o_ref[...] = y
        ob_ref[...] = y.astype(BF16)


def _mlp_ln(xb, x, w_up, w_down, g, b, *, tm=512, tf=1024):
    m, d = x.shape
    ff = w_up.shape[1]
    tm = min(tm, m)
    row = lambda i, j: (i, 0)
    fixed = lambda i, j: (0, 0)
    return pl.pallas_call(
        _mlp_kernel,
        grid=(m // tm, ff // tf),
        in_specs=[pl.BlockSpec((tm, d), row), pl.BlockSpec((tm, d), row),
                  pl.BlockSpec((d, tf), lambda i, j: (0, j)),
                  pl.BlockSpec((tf, d), lambda i, j: (j, 0)),
                  pl.BlockSpec((1, d), fixed), pl.BlockSpec((1, d), fixed)],
        out_specs=[pl.BlockSpec((tm, d), row), pl.BlockSpec((tm, d), row)],
        out_shape=[jax.ShapeDtypeStruct((m, d), F32), jax.ShapeDtypeStruct((m, d), BF16)],
        scratch_shapes=[pltpu.VMEM((tm, d), F32)],
        compiler_params=_params("parallel", "arbitrary"),
        name="mlp_postnorm",
    )(xb, x, w_up, w_down, g.reshape(1, d), b.reshape(1, d))


def _mem_attn_kernel(x_ref, wq_ref, kv_ref, o_ref):
    q = _dot(x_ref[0], wq_ref[...]).astype(BF16)
    scale = MEM_HEAD_DIM ** -0.5
    for h in range(MEM_HEADS):
        lo = h * MEM_HEAD_DIM
        k = kv_ref[0, :, lo:lo + MEM_HEAD_DIM]
        v = kv_ref[0, :, D_MODEL + lo:D_MODEL + lo + MEM_HEAD_DIM]
        s = _dot_nt(q[:, lo:lo + MEM_HEAD_DIM], k) * scale
        p = jnp.exp(s - jnp.max(s, axis=-1, keepdims=True))
        p = p / jnp.sum(p, axis=-1, keepdims=True)
        o_ref[0, :, lo:lo + MEM_HEAD_DIM] = _dot(p.astype(BF16), v).astype(o_ref.dtype)


def _mem_attn(xb, wq, kvm, *, tm=512):
    bsz, seq, d = xb.shape
    n_mem = kvm.shape[1]
    tm = min(tm, seq)
    return pl.pallas_call(
        _mem_attn_kernel,
        grid=(bsz, seq // tm),
        in_specs=[pl.BlockSpec((1, tm, d), lambda b, i: (b, i, 0)),
                  pl.BlockSpec((d, d), lambda b, i: (0, 0)),
                  pl.BlockSpec((1, n_mem, 2 * d), lambda b, i: (b, 0, 0))],
        out_specs=pl.BlockSpec((1, tm, d), lambda b, i: (b, i, 0)),
        out_shape=jax.ShapeDtypeStruct((bsz, seq, d), BF16),
        compiler_params=_params("parallel", "arbitrary"),
        name="mem_attn",
    )(xb, wq, kvm)


def _s5_tables(a_re, a_im, log_dt, b_re, b_im, c_re, c_im, n_chunks):
    hi = lax.Precision.HIGHEST
    G, P, GS, T = S5_GROUPS, S5_STATE, S5_GROUP, S5_CHUNK
    dt = jnp.exp(log_dt.astype(F32))[:, None]
    lr, li = a_re.astype(F32), a_im.astype(F32)
    mag = jnp.exp(lr * dt)
    ab_re, ab_im = mag * jnp.cos(li * dt), mag * jnp.sin(li * dt)
    den = lr * lr + li * li
    nr, ni = ab_re - 1.0, ab_im
    f_re = (nr * lr + ni * li) / den
    f_im = (ni * lr - nr * li) / den
    br, bi = b_re.astype(F32), b_im.astype(F32)
    bb_re = f_re[..., None] * br - f_im[..., None] * bi
    bb_im = f_re[..., None] * bi + f_im[..., None] * br

    def apow(n):
        n = jnp.asarray(n, F32)[:, None, None]
        m = jnp.exp(lr * dt * n)
        return m * jnp.cos(li * dt * n), m * jnp.sin(li * dt * n)

    pw_re, pw_im = apow(np.arange(T + 1))
    abb_re = pw_re[..., None] * bb_re - pw_im[..., None] * bb_im
    abb_im = pw_re[..., None] * bb_im + pw_im[..., None] * bb_re

    rev = np.arange(T - 1, -1, -1)
    w_inc = jnp.concatenate([abb_re[rev], abb_im[rev]], axis=2)
    w_inc = w_inc.transpose(1, 0, 3, 2).reshape(G, T * GS, 2 * P)

    cr, ci = c_re.astype(F32), c_im.astype(F32)
    kern = (jnp.einsum("ghp,ngpk->nghk", cr, abb_re[:T], precision=hi)
            - jnp.einsum("ghp,ngpk->nghk", ci, abb_im[:T], precision=hi))
    tau = np.arange(T)[None, :] - np.arange(T)[:, None]
    toep = jnp.where((tau >= 0)[:, :, None, None, None], kern[np.clip(tau, 0, T - 1)], 0.0)
    w_toep = toep.transpose(2, 0, 4, 1, 3).reshape(G, T * GS, T * GS)

    pr, pi = pw_re[1:], pw_im[1:]
    car_re = cr[None] * pr[:, :, None, :] - ci[None] * pi[:, :, None, :]
    car_im = cr[None] * pi[:, :, None, :] + ci[None] * pr[:, :, None, :]
    w_car = jnp.concatenate([car_re, -car_im], axis=3)
    w_car = w_car.transpose(1, 3, 0, 2).reshape(G, 2 * P, T * GS)

    n_steps = max(1, int(math.ceil(math.log2(n_chunks))))
    sr, si = apow(T * (2 ** np.arange(n_steps)))
    apow_a = jnp.concatenate([sr, sr], axis=2).transpose(1, 0, 2)
    apow_b = jnp.concatenate([-si, si], axis=2).transpose(1, 0, 2)
    return w_inc.astype(BF16), w_toep.astype(BF16), w_car.astype(BF16), apow_a, apow_b


def _s5_kernel(u_ref, winc_ref, wtoep_ref, wcar_ref, pa_ref, pb_ref, d_ref, o_ref, *, n_steps):
    T, GS, GB = S5_CHUNK, S5_GROUP, S5_GROUP_BLOCK
    n_chunks = u_ref.shape[1] // T
    rows = lax.broadcasted_iota(jnp.int32, (n_chunks, 2 * S5_STATE), 0)
    xs = [u_ref[0, pl.ds(j, n_chunks, stride=T), :] for j in range(T)]
    ys = []
    for g in range(GB):
        ub = jnp.concatenate([x[:, g * GS:(g + 1) * GS] for x in xs], axis=1).astype(BF16)
        s = _dot(ub, winc_ref[g])
        for k in range(n_steps):
            d = 1 << k
            sh = jnp.where(rows >= d, pltpu.roll(s, d, 0), 0.0)
            sw = pltpu.roll(sh, S5_STATE, 1)
            s = s + sh * pa_ref[g, k:k + 1, :] + sw * pb_ref[g, k:k + 1, :]
        s_prev = jnp.where(rows >= 1, pltpu.roll(s, 1, 0), 0.0)
        ys.append(_dot(ub, wtoep_ref[g]) + _dot(s_prev.astype(BF16), wcar_ref[g]))
    for t in range(T):
        y = jnp.concatenate([yg[:, t * GS:(t + 1) * GS] for yg in ys], axis=1)
        y = y + d_ref[...] * xs[t]
        o_ref[0, pl.ds(t, n_chunks, stride=T), :] = jax.nn.gelu(y)


def _s5_core(u, tables, d_skip):
    bsz, seq, d = u.shape
    w_inc, w_toep, w_car, apow_a, apow_b = tables
    n_steps = apow_a.shape[1]
    gb = S5_GROUP_BLOCK
    blk = lambda b, i: (b, 0, i)
    wblk = lambda b, i: (i, 0, 0)
    return pl.pallas_call(
        functools.partial(_s5_kernel, n_steps=n_steps),
        grid=(bsz, S5_GROUPS // gb),
        in_specs=[pl.BlockSpec((1, seq, LANES), blk),
                  pl.BlockSpec((gb,) + w_inc.shape[1:], wblk),
                  pl.BlockSpec((gb,) + w_toep.shape[1:], wblk),
                  pl.BlockSpec((gb,) + w_car.shape[1:], wblk),
                  pl.BlockSpec((gb,) + apow_a.shape[1:], wblk),
                  pl.BlockSpec((gb,) + apow_b.shape[1:], wblk),
                  pl.BlockSpec((1, LANES), lambda b, i: (0, i))],
        out_specs=pl.BlockSpec((1, seq, LANES), blk),
        out_shape=jax.ShapeDtypeStruct(u.shape, F32),
        compiler_params=_params("parallel", "arbitrary"),
        name="s5_scan",
    )(u, w_inc, w_toep, w_car, apow_a, apow_b, d_skip.astype(F32).reshape(1, d))


def _kv_proj_kernel(x_ref, w_ref, o_ref):
    acc = _dot(x_ref[0], w_ref[...])
    for s in range(o_ref.shape[1]):
        o_ref[0, s] = acc[:, s * NSA_HEAD_DIM:(s + 1) * NSA_HEAD_DIM].astype(o_ref.dtype)


def _kv_proj(xb, w, out_dtype, *, tm=1024, slabs=8):
    bsz, seq, d = xb.shape
    n_slabs = w.shape[1] // NSA_HEAD_DIM
    tm = min(tm, seq)
    return pl.pallas_call(
        _kv_proj_kernel,
        grid=(bsz, seq // tm, n_slabs // slabs),
        in_specs=[pl.BlockSpec((1, tm, d), lambda b, i, j: (b, i, 0)),
                  pl.BlockSpec((d, slabs * NSA_HEAD_DIM), lambda b, i, j: (0, j))],
        out_specs=pl.BlockSpec((1, slabs, tm, NSA_HEAD_DIM), lambda b, i, j: (b, j, i, 0)),
        out_shape=jax.ShapeDtypeStruct((bsz, n_slabs, seq, NSA_HEAD_DIM), out_dtype),
        compiler_params=_params("parallel", "parallel", "arbitrary"),
        name="kv_proj",
    )(xb, w)


def _compress_kernel(t_ref, pos_ref, w1_ref, w2_ref, o_ref):
    t = t_ref[0, 0]
    n = t.shape[0]
    half = CMP_STRIDE * NSA_HEAD_DIM
    first = _dot((t + pos_ref[0, 0:1, :]).astype(BF16), w1_ref[0, :half, :])
    second = _dot((t + pos_ref[0, 1:2, :]).astype(BF16), w1_ref[0, half:, :])
    pre = first + pltpu.roll(second, n - 1, 0)
    out = _dot(jax.nn.gelu(pre).astype(BF16), w2_ref[0])
    rows = lax.broadcasted_iota(jnp.int32, out.shape, 0)
    o_ref[0, 0] = jnp.where(rows < n - 1, out, 0.0).astype(o_ref.dtype)


def _compress(kv_cmp, pos, w1, w2):
    bsz, n_slots, seq, hd = kv_cmp.shape
    n16 = seq // CMP_STRIDE
    t16 = kv_cmp.reshape(bsz, n_slots, n16, CMP_STRIDE * hd)
    kvh = NSA_KV_HEADS
    return pl.pallas_call(
        _compress_kernel,
        grid=(bsz, n_slots),
        in_specs=[pl.BlockSpec((1, 1, n16, CMP_STRIDE * hd), lambda b, s: (b, s, 0, 0)),
                  pl.BlockSpec((1, 2, CMP_STRIDE * hd), lambda b, s: (s // kvh, 0, 0)),
                  pl.BlockSpec((1, CMP_LEN * hd, hd), lambda b, s: (s // kvh, 0, 0)),
                  pl.BlockSpec((1, hd, hd), lambda b, s: (s // kvh, 0, 0))],
        out_specs=pl.BlockSpec((1, 1, n16, hd), lambda b, s: (b, s, 0, 0)),
        out_shape=jax.ShapeDtypeStruct((bsz, n_slots, n16, hd), BF16),
        compiler_params=_params("parallel", "arbitrary"),
        name="nsa_compress",
    )(t16, pos, w1, w2)


def _nsa_kernel(q_ref, g_ref, kc_ref, vc_ref, ks_ref, vst_ref, kw_ref, vw_ref, o_ref,
                kaug_ref, band_ref, selmap_ref):
    qb = pl.program_id(2)
    s0 = qb * Q_BLOCK
    R, hd, QB = NSA_REP, NSA_HEAD_DIM, Q_BLOCK
    rows = R * QB
    n_cmp = kc_ref.shape[2]
    seq = ks_ref.shape[2]
    n_sel = seq // SEL_LEN
    n_win = band_ref.shape[1]

    @pl.when(qb == 0)
    def _():
        lane = lax.broadcasted_iota(jnp.int32, (SEL_TILE, n_sel), 1)
        blk = lax.broadcasted_iota(jnp.int32, (SEL_TILE, n_sel), 0) // SEL_LEN

        def fill(j, _):
            base = pl.multiple_of(j * SEL_TILE, SEL_TILE)
            onehot = jnp.where(lane == blk + j * (SEL_TILE // SEL_LEN), 1.0, 0.0).astype(BF16)
            kaug_ref[pl.ds(base, SEL_TILE), :] = jnp.concatenate(
                [ks_ref[0, 0, pl.ds(base, SEL_TILE), :], onehot], axis=1)
            return 0

        lax.fori_loop(0, seq // SEL_TILE, fill, 0)
        qi = lax.broadcasted_iota(jnp.int32, (QB, n_win), 0)
        col = lax.broadcasted_iota(jnp.int32, (QB, n_win), 1)
        band_ref[...] = jnp.where((col > qi) & (col <= qi + WINDOW), 0.0, NEG_BIG)
        sel_s = lax.broadcasted_iota(jnp.int32, (n_sel, n_cmp), 0) * SEL_LEN
        cmp_c = lax.broadcasted_iota(jnp.int32, (n_sel, n_cmp), 1) * CMP_STRIDE
        selmap_ref[...] = jnp.where((cmp_c < sel_s + SEL_LEN) & (cmp_c + CMP_LEN > sel_s),
                                    1.0, 0.0).astype(BF16)

    q4 = jnp.concatenate([q_ref[0, :, r * hd:(r + 1) * hd] for r in range(R)], axis=0)
    t_rows = s0 + lax.broadcasted_iota(jnp.int32, (rows, 1), 0) % QB

    cmp_end =lax.broadcasted_iota(jnp.int32, (1, n_cmp), 1) * CMP_STRIDE + (CMP_LEN - 1)
    sc = jnp.where(cmp_end <= t_rows, _dot_nt(q4, kc_ref[0, 0]), NEG_BIG)
    mc = jnp.max(sc, axis=-1, keepdims=True)
    pc = jnp.exp(sc - mc)
    lc = jnp.sum(pc, axis=-1, keepdims=True)
    pc = pc * jnp.where(mc > 0.5 * NEG_BIG, 1.0 / jnp.maximum(lc, 1e-30), 0.0)
    o_cmp = _dot(pc.astype(BF16), vc_ref[0, 0])

    pc_sum = pc[0:QB]
    for r in range(1, R):
        pc_sum = pc_sum + pc[r * QB:(r + 1) * QB]
    pc_hi = pc_sum.astype(BF16)
    pc_lo = (pc_sum - pc_hi.astype(F32)).astype(BF16)
    sel_map_t = selmap_ref[...]
    imp =_dot_nt(sel_map_t, pc_hi) + _dot_nt(sel_map_t, pc_lo)

    sid = lax.broadcasted_iota(jnp.int32, (n_sel, QB), 0)
    sid_f = sid.astype(F32)
    tq = s0 + lax.broadcasted_iota(jnp.int32, (n_sel, QB), 1)
    cur = tq // SEL_LEN
    forced = (sid == 0) | (sid == cur) | (sid == cur - 1)
    imp = jnp.where(forced, SEL_FORCE, jnp.where(sid * SEL_LEN <= tq, imp, -SEL_FORCE))

    top_n = min(SEL_TOPN, n_sel)
    sel = jnp.zeros((n_sel, QB), F32)
    for _ in range(top_n):
        best = jnp.max(imp, axis=0, keepdims=True)
        first = jnp.min(jnp.where(imp == best, sid_f, float(n_sel)), axis=0, keepdims=True)
        hit = sid_f == first
        sel = jnp.where(hit, 1.0, sel)
        imp = jnp.where(hit, -jnp.inf, imp)
    bias_t = jnp.where(sel > 0.5, 0.0, NEG_BIG).astype(BF16)
    q_t = [q_ref[0, :, r * hd:(r + 1) * hd].astype(F32).T.astype(BF16) for r in range(R)]
    q_aug_t = jnp.concatenate([jnp.concatenate(q_t, axis=1),
                               jnp.concatenate([bias_t] * R, axis=1)], axis=0)
    t_lanes = s0 + lax.broadcasted_iota(jnp.int32, (1, rows), 1) % QB

    def sel_tile(j, carry, causal):
        m_prev, l_prev, acc = carry
        base = pl.multiple_of(j * SEL_TILE, SEL_TILE)
        s = _dot(kaug_ref[pl.ds(base, SEL_TILE), :], q_aug_t)
        if causal:
            kpos = base + lax.broadcasted_iota(jnp.int32, (SEL_TILE, 1), 0)
            s = jnp.where(kpos <= t_lanes, s, NEG_BIG)
        m_new = jnp.maximum(m_prev, jnp.max(s, axis=0, keepdims=True))
        corr = jnp.exp(m_prev - m_new)
        p = jnp.exp(s - m_new)
        l_new = l_prev * corr + jnp.sum(p, axis=0, keepdims=True)
        acc = acc * corr + _dot(vst_ref[0, 0, :, pl.ds(base, SEL_TILE)], p.astype(BF16))
        return m_new, l_new, acc

    n_past = s0 // SEL_TILE
    init = (jnp.full((1, rows), NEG_BIG, F32), jnp.zeros((1, rows), F32), jnp.zeros((hd, rows), F32))
    carry = lax.fori_loop(0, n_past, functools.partial(sel_tile, causal=False), init)
    _, l_sel, acc_sel = sel_tile(n_past, carry, True)
    o_slc_t = acc_sel * (1.0 / l_sel)
    o_slc = jnp.concatenate([o_slc_t[:, r * QB:(r + 1) * QB].T for r in range(R)], axis=0)

    start = pl.multiple_of(s0, QB)
    col = lax.broadcasted_iota(jnp.int32, (1, n_win), 1)
    win_bias = band_ref[...] + jnp.where(col + s0 >= WINDOW, 0.0, NEG_BIG)
    sw = _dot_nt(q4, kw_ref[0, 0, pl.ds(start, n_win), :])
    sw = jnp.concatenate([sw[r * QB:(r + 1) * QB] + win_bias for r in range(R)], axis=0)
    pw = jnp.exp(sw - jnp.max(sw, axis=-1, keepdims=True))
    lw = jnp.sum(pw, axis=-1, keepdims=True)
    o_win = _dot(pw.astype(BF16), vw_ref[0, 0, pl.ds(start, n_win), :]) * (1.0 / lw)

    gates = g_ref[0]
    for r in range(R):
        sl = slice(r * QB, (r + 1) * QB)
        c = r * N_GATES
        out = (gates[:, c:c + 1] * o_cmp[sl] + gates[:, c + 1:c + 2] * o_slc[sl]
               + gates[:, c + 2:c + 3] * o_win[sl])
        o_ref[0, :, r * hd:(r + 1) * hd] = out.astype(o_ref.dtype)


def _nsa_attention(q, gates, kv_cmp, kv_sel, kv_win):
    bsz, seq, _ = q.shape
    G, hd = NSA_KV_HEADS, NSA_HEAD_DIM
    n_cmp = kv_cmp.shape[2]
    gw = NSA_REP * hd
    n_win = WINDOW + Q_BLOCK
    v_sel_t = jnp.swapaxes(kv_sel[:, G:2 * G], 2, 3)

    def slot(k):
        return lambda b, g, i: (b, k * G + g, 0, 0)

    return pl.pallas_call(
        _nsa_kernel,
        grid=(bsz, G, seq // Q_BLOCK),
        in_specs=[pl.BlockSpec((1, Q_BLOCK, gw), lambda b, g, i: (b, i, g)),
                  pl.BlockSpec((1, Q_BLOCK, LANES), lambda b, g, i: (b, i, g)),
                  pl.BlockSpec((1, 1, n_cmp, hd), slot(0)),
                  pl.BlockSpec((1, 1, n_cmp, hd), slot(1)),
                  pl.BlockSpec((1, 1, seq, hd), slot(0)),
                  pl.BlockSpec((1, 1, hd, seq), lambda b, g, i: (b, g, 0, 0)),
                  pl.BlockSpec((1, 1, WINDOW + seq, hd), slot(0)),
                  pl.BlockSpec((1, 1, WINDOW + seq, hd), slot(1))],
        out_specs=pl.BlockSpec((1, Q_BLOCK, gw), lambda b, g, i: (b, i, g)),
        out_shape=jax.ShapeDtypeStruct(q.shape, BF16),
        scratch_shapes=[pltpu.VMEM((seq, hd + seq // SEL_LEN), BF16),
                        pltpu.VMEM((Q_BLOCK, n_win), F32),
                        pltpu.VMEM((seq // SEL_LEN, n_cmp), BF16)],
        compiler_params=_params("arbitrary", "arbitrary", "arbitrary"),
        name="nsa_attention",
    )(q, gates, kv_cmp, kv_cmp, kv_sel, v_sel_t, kv_win, kv_win)


def _s5_layer(x, xin, bsz, seq, w_in, tables, d_skip, w_glu, w_out, g, b):
    m = bsz * seq
    u = _mm(xin, w_in, F32, tm=1024, tn=1024)
    y = _s5_core(u.reshape(bsz, seq, D_MODEL), tables, d_skip)
    z = _mm_glu(y.reshape(m, D_MODEL), w_glu)
    return _mm_res_ln(z, w_out, x, g, b)


def _nsa_layer(x, xb, bsz, seq, kv_cmp, kv_sel, kv_win, w_q, w_gate, w_o, g, b):
    m = bsz * seq
    q = _mm(xb, w_q, BF16, tm=1024, tn=1024, scale=NSA_HEAD_DIM ** -0.5)
    gates = _mm(xb, w_gate, F32, tm=1024, tn=512, act="sigmoid")
    o = _nsa_attention(q.reshape(bsz, seq, -1), gates.reshape(bsz, seq, -1), kv_cmp, kv_sel, kv_win)
    return _mm_res_ln(o.reshape(m, -1), w_o, x, g, b)


def _gate_weights(w_qg):
    n_q = NSA_HEADS * NSA_HEAD_DIM
    per = NSA_REP * N_GATES
    wg = w_qg[:, n_q:].reshape(-1, NSA_KV_HEADS, per)
    wg = jnp.pad(wg, ((0, 0), (0, 0), (0, LANES - per)))
    return wg.reshape(-1, NSA_KV_HEADS * LANES).astype(BF16)


def kernel(x, mem, s5_w_in, s5_a_re, s5_a_im, s5_log_dt, s5_b_re, s5_b_im, s5_c_re, s5_c_im, s5_d, s5_w_glu, s5_w_out, kv_w, cmp_pos_k, cmp_w1_k, cmp_w2_k, cmp_pos_v, cmp_w1_v, cmp_w2_v, nsa_w_qg, nsa_w_o, mem_w_q, mem_w_kv, mem_w_o, mlp_w_up, mlp_w_down, ln_g, ln_b):
    bsz, seq, d = x.shape
    m = bsz * seq
    bf = lambda w: w.astype(BF16)
    xf = x.reshape(m, d)
    xin = xf
    memf = mem.reshape(-1, d)
    kv_cmp = kv_sel = kv_win = None
    for layer in range(DEPTH):
        if layer < N_A_LAYERS:
            i = layer
            tables = _s5_tables(s5_a_re[i], s5_a_im[i], s5_log_dt[i], s5_b_re[i], s5_b_im[i],
                                s5_c_re[i], s5_c_im[i], seq // S5_CHUNK)
            xf, xin = _s5_layer(xf, xin, bsz, seq, bf(s5_w_in[i]), tables, s5_d[i], bf(s5_w_glu[i]),
                                bf(s5_w_out[i]), ln_g[layer, 0], ln_b[layer, 0])
        else:
            i = layer - N_A_LAYERS
            if xin.dtype != BF16:
                xin = xin.astype(BF16)
            if i == 0:
                n_cmp_cols = 2 * NSA_KV_HEADS * NSA_HEAD_DIM
                x3 = xin.reshape(bsz, seq, d)
                kv_c = _kv_proj(x3, bf(kv_w[:, :n_cmp_cols]), F32)
                kv_sel = _kv_proj(x3, bf(kv_w[:, n_cmp_cols:]), BF16)
                kv_win = jnp.pad(kv_sel[:, 2 * NSA_KV_HEADS:], ((0, 0), (0, 0), (WINDOW, 0), (0, 0)))
                half = CMP_STRIDE * NSA_HEAD_DIM
                pos = jnp.stack([cmp_pos_k.reshape(2, half), cmp_pos_v.reshape(2, half)])
                kv_cmp = _compress(kv_c, pos, jnp.stack([bf(cmp_w1_k), bf(cmp_w1_v)]),
                                   jnp.stack([bf(cmp_w2_k), bf(cmp_w2_v)]))
            n_q = NSA_HEADS * NSA_HEAD_DIM
            xf, xin = _nsa_layer(xf, xin, bsz, seq, kv_cmp, kv_sel, kv_win, bf(nsa_w_qg[i][:, :n_q]),
                                 _gate_weights(nsa_w_qg[i]), bf(nsa_w_o[i]),
                                 ln_g[layer, 0], ln_b[layer, 0])
        kvm = _mm(memf, bf(mem_w_kv[layer]), BF16, tm=512, tn=1024)
        o = _mem_attn(xin.reshape(bsz, seq, d), bf(mem_w_q[layer]), kvm.reshape(bsz, -1, 2 * d))
        xf, xin = _mm_res_ln(o.reshape(m, d), bf(mem_w_o[layer]), xf, ln_g[layer, 1], ln_b[layer, 1])
        xf, xin = _mlp_ln(xin, xf, bf(mlp_w_up[layer]), bf(mlp_w_down[layer]),
                          ln_g[layer, 2], ln_b[layer, 2])
    return xf.reshape(bsz, seq, d)
```

```python
import functools
import math

import jax
import jax.numpy as jnp
import numpy as np
from jax import lax
from jax.experimental import pallas as pl
from jax.experimental.pallas import tpu as pltpu

F32 = jnp.float32
BF16 = jnp.bfloat16

D_MODEL = 2048
DEPTH = 2
N_A_LAYERS = DEPTH // 2

S5_GROUP = 16
S5_GROUPS = D_MODEL // S5_GROUP
S5_STATE = 64
S5_CHUNK = 16
S5_ROW = S5_CHUNK * S5_GROUP
S5_GROUP_BLOCK = 8

NSA_HEADS = 16
NSA_HEAD_DIM = D_MODEL // NSA_HEADS
NSA_KV_HEADS = 4
NSA_REP = NSA_HEADS // NSA_KV_HEADS
CMP_LEN = 32
CMP_STRIDE = 16
SEL_LEN = 64
SEL_TOPN = 16
WINDOW = 512
Q_BLOCK = 256
N_GATES = 3
N_KV_SLOTS = 6
SEL_TILE = 1024

MEM_HEADS = 4
MEM_HEAD_DIM = D_MODEL // MEM_HEADS

D_FF = 4 * D_MODEL

DN_ALPHA = float((2 * DEPTH) ** 0.25)
LN_EPS = 1e-5
NEG_BIG = -1e30
SEL_FORCE = 1e9

LANES = 128
VMEM_LIMIT = 56 * 1024 * 1024


def _params(*semantics):
    return pltpu.CompilerParams(dimension_semantics=semantics, vmem_limit_bytes=VMEM_LIMIT)


def _layer_norm(y, g, b):
    mu = jnp.mean(y, axis=-1, keepdims=True)
    yc = y - mu
    var = jnp.mean(yc * yc, axis=-1, keepdims=True)
    return yc * lax.rsqrt(var + LN_EPS) * g + b


def _dot(a, b):
    return jnp.dot(a, b, preferred_element_type=F32)


def _dot_nt(a, b):
    return lax.dot_general(a, b, (((1,), (1,)), ((), ())), preferred_element_type=F32)


def _mm_kernel(x_ref, w_ref, o_ref, *, scale, act):
    acc = _dot(x_ref[...].astype(BF16), w_ref[...])
    if scale is not None:
        acc = acc * scale
    if act == "sigmoid":
        acc = jax.nn.sigmoid(acc)
    o_ref[...] = acc.astype(o_ref.dtype)


def _mm(x, w, out_dtype, *, tm, tn, scale=None, act=None):
    m, k = x.shape
    n = w.shape[1]
    tm, tn = min(tm, m), min(tn, n)
    return pl.pallas_call(
        functools.partial(_mm_kernel, scale=scale, act=act),
        grid=(m // tm, n // tn),
        in_specs=[pl.BlockSpec((tm, k), lambda i, j: (i, 0)),
                  pl.BlockSpec((k, tn), lambda i, j: (0, j))],
        out_specs=pl.BlockSpec((tm, tn), lambda i, j: (i, j)),
        out_shape=jax.ShapeDtypeStruct((m, n), out_dtype),
        compiler_params=_params("parallel", "arbitrary"),
        name="mm",
    )(x, w)


def _glu_kernel(y_ref, wv_ref, wg_ref, o_ref):
    y = y_ref[...].astype(BF16)
    val = _dot(y, wv_ref[...])
    gate = _dot(y, wg_ref[...])
    o_ref[...] = (val * jax.nn.sigmoid(gate)).astype(o_ref.dtype)


def _mm_glu(y, w_glu, *, tm=1024, tn=512):
    m, k = y.shape
    n = w_glu.shape[1] // 2
    tm = min(tm, m)
    nb = n // tn
    return pl.pallas_call(
        _glu_kernel,
        grid=(m // tm, nb),
        in_specs=[pl.BlockSpec((tm, k), lambda i, j: (i, 0)),
                  pl.BlockSpec((k, tn), lambda i, j: (0, j)),
                  pl.BlockSpec((k, tn), lambda i, j: (0, j + nb))],
        out_specs=pl.BlockSpec((tm, tn), lambda i, j: (i, j)),
        out_shape=jax.ShapeDtypeStruct((m, n), BF16),
        compiler_params=_params("parallel", "arbitrary"),
        name="mm_glu",
    )(y, w_glu, w_glu)


def _res_ln_kernel(z_ref, w_ref, x_ref, g_ref, b_ref, o_ref, ob_ref):
    h = _dot(z_ref[...], w_ref[...])
    y = _layer_norm(DN_ALPHA * x_ref[...] + h, g_ref[...], b_ref[...])
    o_ref[...] = y
    ob_ref[...] = y.astype(BF16)


def _mm_res_ln(z, w, x, g, b, *, tm=512):
    m, k = z.shape
    d = w.shape[1]
    tm = min(tm, m)
    row = lambda i: (i, 0)
    fixed = lambda i: (0, 0)
    return pl.pallas_call(
        _res_ln_kernel,
        grid=(m // tm,),
        in_specs=[pl.BlockSpec((tm, k), row), pl.BlockSpec((k, d), fixed),
                  pl.BlockSpec((tm, d), row), pl.BlockSpec((1, d), fixed),
                  pl.BlockSpec((1, d), fixed)],
        out_specs=[pl.BlockSpec((tm, d), row), pl.BlockSpec((tm, d), row)],
        out_shape=[jax.ShapeDtypeStruct((m, d), F32), jax.ShapeDtypeStruct((m, d), BF16)],
        compiler_params=_params("parallel"),
        name="proj_postnorm",
    )(z, w, x, g.reshape(1, d), b.reshape(1, d))


def _mlp_kernel(xb_ref, x_ref, wu_ref, wd_ref, g_ref, b_ref, o_ref, ob_ref, acc_ref):
    j = pl.program_id(1)
    h = jnp.maximum(_dot(xb_ref[...], wu_ref[...]), 0.0)
    part = _dot((h * h).astype(BF16), wd_ref[...])

    @pl.when(j == 0)
    def _():
        acc_ref[...] = part

    @pl.when(j > 0)
    def _():
        acc_ref[...] += part

    @pl.when(j == pl.num_programs(1) - 1)
    def _():
        y = _layer_norm(DN_ALPHA * x_ref[...] + acc_ref[...], g_ref[...], b_ref[...])
        o_ref[...] = y
        ob_ref[...] = y.astype(BF16)


def _mlp_ln(xb, x, w_up, w_down, g, b, *, tm=512, tf=1024):
    m, d = x.shape
    ff = w_up.shape[1]
    tm = min(tm, m)
    row = lambda i, j: (i, 0)
    fixed = lambda i, j: (0, 0)
    return pl.pallas_call(
        _mlp_kernel,
        grid=(m // tm, ff // tf),
        in_specs=[pl.BlockSpec((tm, d), row), pl.BlockSpec((tm, d), row),
                  pl.BlockSpec((d, tf), lambda i, j: (0, j)),
                  pl.BlockSpec((tf, d), lambda i, j: (j, 0)),
                  pl.BlockSpec((1, d), fixed), pl.BlockSpec((1, d), fixed)],
        out_specs=[pl.BlockSpec((tm, d), row), pl.BlockSpec((tm, d), row)],
        out_shape=[jax.ShapeDtypeStruct((m, d), F32), jax.ShapeDtypeStruct((m, d), BF16)],
        scratch_shapes=[pltpu.VMEM((tm, d), F32)],
        compiler_params=_params("parallel", "arbitrary"),
        name="mlp_postnorm",
    )(xb, x, w_up, w_down, g.reshape(1, d), b.reshape(1, d))


def _mem_attn_kernel(x_ref, wq_ref, kv_ref, o_ref):
    q = _dot(x_ref[0], wq_ref[...]).astype(BF16)
    scale = MEM_HEAD_DIM ** -0.5
    for h in range(MEM_HEADS):
        lo = h * MEM_HEAD_DIM
        k = kv_ref[0, :, lo:lo + MEM_HEAD_DIM]
        v = kv_ref[0, :, D_MODEL + lo:D_MODEL + lo + MEM_HEAD_DIM]
        s = _dot_nt(q[:, lo:lo + MEM_HEAD_DIM], k) * scale
        p = jnp.exp(s - jnp.max(s, axis=-1, keepdims=True))
        p = p / jnp.sum(p, axis=-1, keepdims=True)
        o_ref[0, :, lo:lo + MEM_HEAD_DIM] = _dot(p.astype(BF16), v).astype(o_ref.dtype)


def _mem_attn(xb, wq, kvm, *, tm=512):
    bsz, seq, d = xb.shape
    n_mem = kvm.shape[1]
    tm = min(tm, seq)
    return pl.pallas_call(
        _mem_attn_kernel,
        grid=(bsz, seq // tm),
        in_specs=[pl.BlockSpec((1, tm, d), lambda b, i: (b, i, 0)),
                  pl.BlockSpec((d, d), lambda b, i: (0, 0)),
                  pl.BlockSpec((1, n_mem, 2 * d), lambda b, i: (b, 0, 0))],
        out_specs=pl.BlockSpec((1, tm, d), lambda b, i: (b, i, 0)),
        out_shape=jax.ShapeDtypeStruct((bsz, seq, d), BF16),
        compiler_params=_params("parallel", "arbitrary"),
        name="mem_attn",
    )(xb, wq, kvm)


def _s5_tables(a_re, a_im, log_dt, b_re, b_im, c_re, c_im, n_chunks):
    hi = lax.Precision.HIGHEST
    G, P, GS, T = S5_GROUPS, S5_STATE, S5_GROUP, S5_CHUNK
    dt = jnp.exp(log_dt.astype(F32))[:, None]
    lr, li = a_re.astype(F32), a_im.astype(F32)
    mag = jnp.exp(lr * dt)
    ab_re, ab_im = mag * jnp.cos(li * dt), mag * jnp.sin(li * dt)
    den = lr * lr + li * li
    nr, ni = ab_re - 1.0, ab_im
    f_re = (nr * lr + ni * li) / den
    f_im = (ni * lr - nr * li) / den
    br, bi = b_re.astype(F32), b_im.astype(F32)
    bb_re = f_re[..., None] * br - f_im[..., None] * bi
    bb_im = f_re[..., None] * bi + f_im[..., None] * br

    def apow(n):
        n = jnp.asarray(n, F32)[:, None, None]
        m = jnp.exp(lr * dt * n)
        return m * jnp.cos(li * dt * n), m * jnp.sin(li * dt * n)

    pw_re, pw_im = apow(np.arange(T + 1))
    abb_re = pw_re[..., None] * bb_re - pw_im[..., None] * bb_im
    abb_im = pw_re[..., None] * bb_im + pw_im[..., None] * bb_re

    rev = np.arange(T - 1, -1, -1)
    w_inc = jnp.concatenate([abb_re[rev], abb_im[rev]], axis=2)
    w_inc = w_inc.transpose(1, 0, 3, 2).reshape(G, T * GS, 2 * P)

    cr, ci = c_re.astype(F32), c_im.astype(F32)
    kern = (jnp.einsum("ghp,ngpk->nghk", cr, abb_re[:T], precision=hi)
            - jnp.einsum("ghp,ngpk->nghk", ci, abb_im[:T], precision=hi))
    tau = np.arange(T)[None, :] - np.arange(T)[:, None]
    toep = jnp.where((tau >= 0)[:, :, None, None, None], kern[np.clip(tau, 0, T - 1)], 0.0)
    w_toep = toep.transpose(2, 0, 4, 1, 3).reshape(G, T * GS, T * GS)

    pr, pi = pw_re[1:], pw_im[1:]
    car_re = cr[None] * pr[:, :, None, :] - ci[None] * pi[:, :, None, :]
    car_im = cr[None] * pi[:, :, None, :] + ci[None] * pr[:, :, None, :]
    w_car = jnp.concatenate([car_re, -car_im], axis=3)
    w_car = w_car.transpose(1, 3, 0, 2).reshape(G, 2 * P, T * GS)

    n_steps = max(1, int(math.ceil(math.log2(n_chunks))))
    sr, si = apow(T * (2 ** np.arange(n_steps)))
    apow_a = jnp.concatenate([sr, sr], axis=2).transpose(1, 0, 2)
    apow_b = jnp.concatenate([-si, si], axis=2).transpose(1, 0, 2)
    return w_inc.astype(BF16), w_toep.astype(BF16), w_car.astype(BF16), apow_a, apow_b


def _s5_kernel(u_ref, winc_ref, wtoep_ref, wcar_ref, pa_ref, pb_ref, d_ref, o_ref, *, n_steps):
    T, GS, GB = S5_CHUNK, S5_GROUP, S5_GROUP_BLOCK
    n_chunks = u_ref.shape[1] // T
    rows = lax.broadcasted_iota(jnp.int32, (n_chunks, 2 * S5_STATE), 0)
    xs = [u_ref[0, pl.ds(j, n_chunks, stride=T), :] for j in range(T)]
    ys = []
    for g in range(GB):
        ub = jnp.concatenate([x[:, g * GS:(g + 1) * GS] for x in xs], axis=1).astype(BF16)
        s = _dot(ub, winc_ref[g])
        for k in range(n_steps):
            d = 1 << k
            sh = jnp.where(rows >= d, pltpu.roll(s, d, 0), 0.0)
            sw = pltpu.roll(sh, S5_STATE, 1)
            s = s + sh * pa_ref[g, k:k + 1, :] + sw * pb_ref[g, k:k + 1, :]
        s_prev = jnp.where(rows >= 1, pltpu.roll(s, 1, 0), 0.0)
        ys.append(_dot(ub, wtoep_ref[g]) + _dot(s_prev.astype(BF16), wcar_ref[g]))
    for t in range(T):
        y = jnp.concatenate([yg[:, t * GS:(t + 1) * GS] for yg in ys], axis=1)
        y = y + d_ref[...] * xs[t]
        o_ref[0, pl.ds(t, n_chunks, stride=T), :] = jax.nn.gelu(y)


def _s5_core(u, tables, d_skip):
    bsz, seq, d = u.shape
    w_inc, w_toep, w_car, apow_a, apow_b = tables
    n_steps = apow_a.shape[1]
    gb = S5_GROUP_BLOCK
    blk = lambda b, i: (b, 0, i)
    wblk = lambda b, i: (i, 0, 0)
    return pl.pallas_call(
        functools.partial(_s5_kernel, n_steps=n_steps),
        grid=(bsz, S5_GROUPS // gb),
        in_specs=[pl.BlockSpec((1, seq, LANES), blk),
                  pl.BlockSpec((gb,) + w_inc.shape[1:], wblk),
                  pl.BlockSpec((gb,) + w_toep.shape[1:], wblk),
                  pl.BlockSpec((gb,) + w_car.shape[1:], wblk),
                  pl.BlockSpec((gb,) + apow_a.shape[1:], wblk),
                  pl.BlockSpec((gb,) + apow_b.shape[1:], wblk),
                  pl.BlockSpec((1, LANES), lambda b, i: (0, i))],
        out_specs=pl.BlockSpec((1, seq, LANES), blk),
        out_shape=jax.ShapeDtypeStruct(u.shape, F32),
        compiler_params=_params("parallel", "arbitrary"),
        name="s5_scan",
    )(u, w_inc, w_toep, w_car, apow_a, apow_b, d_skip.astype(F32).reshape(1, d))


def _kv_proj_kernel(x_ref, w_ref, o_ref):
    acc = _dot(x_ref[0], w_ref[...])
    for s in range(o_ref.shape[1]):
        o_ref[0, s] = acc[:, s * NSA_HEAD_DIM:(s + 1) * NSA_HEAD_DIM].astype(o_ref.dtype)


def _kv_proj(xb, w, out_dtype, *, tm=1024, slabs=8):
    bsz, seq, d = xb.shape
    n_slabs = w.shape[1] // NSA_HEAD_DIM
    tm = min(tm, seq)
    return pl.pallas_call(
        _kv_proj_kernel,
        grid=(bsz, seq // tm, n_slabs // slabs),
        in_specs=[pl.BlockSpec((1, tm, d), lambda b, i, j: (b, i, 0)),
                  pl.BlockSpec((d, slabs * NSA_HEAD_DIM), lambda b, i, j: (0, j))],
        out_specs=pl.BlockSpec((1, slabs, tm, NSA_HEAD_DIM), lambda b, i, j: (b, j, i, 0)),
        out_shape=jax.ShapeDtypeStruct((bsz, n_slabs, seq, NSA_HEAD_DIM), out_dtype),
        compiler_params=_params("parallel", "parallel", "arbitrary"),
        name="kv_proj",
    )(xb, w)


def _compress_kernel(t_ref, pos_ref, w1_ref, w2_ref, o_ref):
    t = t_ref[0, 0]
    n = t.shape[0]
    half = CMP_STRIDE * NSA_HEAD_DIM
    first = _dot((t + pos_ref[0, 0:1, :]).astype(BF16), w1_ref[0, :half, :])
    second = _dot((t + pos_ref[0, 1:2, :]).astype(BF16), w1_ref[0, half:, :])
    pre = first + pltpu.roll(second, n - 1, 0)
    out = _dot(jax.nn.gelu(pre).astype(BF16), w2_ref[0])
    rows = lax.broadcasted_iota(jnp.int32, out.shape, 0)
    o_ref[0, 0] = jnp.where(rows < n - 1, out, 0.0).astype(o_ref.dtype)


def _compress(kv_cmp, pos, w1, w2):
    bsz, n_slots, seq, hd = kv_cmp.shape
    n16 = seq // CMP_STRIDE
    t16 = kv_cmp.reshape(bsz, n_slots, n16, CMP_STRIDE * hd)
    kvh = NSA_KV_HEADS
    return pl.pallas_call(
        _compress_kernel,
        grid=(bsz, n_slots),
        in_specs=[pl.BlockSpec((1, 1, n16, CMP_STRIDE * hd), lambda b, s: (b, s, 0, 0)),
                  pl.BlockSpec((1, 2, CMP_STRIDE * hd), lambda b, s: (s // kvh, 0, 0)),
                  pl.BlockSpec((1, CMP_LEN * hd, hd), lambda b, s: (s // kvh, 0, 0)),
                  pl.BlockSpec((1, hd, hd), lambda b, s: (s // kvh, 0, 0))],
        out_specs=pl.BlockSpec((1, 1, n16, hd), lambda b, s: (b, s, 0, 0)),
        out_shape=jax.ShapeDtypeStruct((bsz, n_slots, n16, hd), BF16),
        compiler_params=_params("parallel", "arbitrary"),
        name="nsa_compress",
    )(t16, pos, w1, w2)


def _nsa_kernel(q_ref, g_ref, kc_ref, vc_ref, ks_ref, vst_ref, kw_ref, vw_ref, o_ref,
                kaug_ref, band_ref, selmap_ref):
    qb = pl.program_id(2)
    s0 = qb * Q_BLOCK
    R, hd, QB = NSA_REP, NSA_HEAD_DIM, Q_BLOCK
    rows = R * QB
    n_cmp = kc_ref.shape[2]
    seq = ks_ref.shape[2]
    n_sel = seq // SEL_LEN
    n_win = band_ref.shape[1]

    @pl.when(qb == 0)
    def _():
        lane = lax.broadcasted_iota(jnp.int32, (SEL_TILE, n_sel), 1)
        blk = lax.broadcasted_iota(jnp.int32, (SEL_TILE, n_sel), 0) // SEL_LEN

        def fill(j, _):
            base = pl.multiple_of(j * SEL_TILE, SEL_TILE)
            onehot = jnp.where(lane == blk + j * (SEL_TILE // SEL_LEN), 1.0, 0.0).astype(BF16)
            kaug_ref[pl.ds(base, SEL_TILE), :] = jnp.concatenate(
                [ks_ref[0, 0, pl.ds(base, SEL_TILE), :], onehot], axis=1)
            return 0

        lax.fori_loop(0, seq // SEL_TILE, fill, 0)
        qi = lax.broadcasted_iota(jnp.int32, (QB, n_win), 0)
        col = lax.broadcasted_iota(jnp.int32, (QB, n_win), 1)
        band_ref[...] = jnp.where((col > qi) & (col <= qi + WINDOW), 0.0, NEG_BIG)
        sel_s = lax.broadcasted_iota(jnp.int32, (n_sel, n_cmp), 0) * SEL_LEN
        cmp_c = lax.broadcasted_iota(jnp.int32, (n_sel, n_cmp), 1) * CMP_STRIDE
        selmap_ref[...] = jnp.where((cmp_c < sel_s + SEL_LEN) & (cmp_c + CMP_LEN > sel_s),
                                    1.0, 0.0).astype(BF16)

    q4 = jnp.concatenate([q_ref[0, :, r * hd:(r + 1) * hd] for r in range(R)], axis=0)
    t_rows = s0 + lax.broadcasted_iota(jnp.int32, (rows, 1), 0) % QB

    cmp_end =lax.broadcasted_iota(jnp.int32, (1, n_cmp), 1) * CMP_STRIDE + (CMP_LEN - 1)
    sc = jnp.where(cmp_end <= t_rows, _dot_nt(q4, kc_ref[0, 0]), NEG_BIG)
    mc = jnp.max(sc, axis=-1, keepdims=True)
    pc = jnp.exp(sc - mc)
    lc = jnp.sum(pc, axis=-1, keepdims=True)
    pc = pc * jnp.where(mc > 0.5 * NEG_BIG, 1.0 / jnp.maximum(lc, 1e-30), 0.0)
    o_cmp = _dot(pc.astype(BF16), vc_ref[0, 0])

    pc_sum = pc[0:QB]
    for r in range(1, R):
        pc_sum = pc_sum + pc[r * QB:(r + 1) * QB]
    pc_hi = pc_sum.astype(BF16)
    pc_lo = (pc_sum - pc_hi.astype(F32)).astype(BF16)
    sel_map_t = selmap_ref[...]
    imp =_dot_nt(sel_map_t, pc_hi) + _dot_nt(sel_map_t, pc_lo)

    sid = lax.broadcasted_iota(jnp.int32, (n_sel, QB), 0)
    sid_f = sid.astype(F32)
    tq = s0 + lax.broadcasted_iota(jnp.int32, (n_sel, QB), 1)
    cur = tq // SEL_LEN
    forced = (sid == 0) | (sid == cur) | (sid == cur - 1)
    imp = jnp.where(forced, SEL_FORCE, jnp.where(sid * SEL_LEN <= tq, imp, -SEL_FORCE))

    top_n = min(SEL_TOPN, n_sel)
    sel = jnp.zeros((n_sel, QB), F32)
    for _ in range(top_n):
        best = jnp.max(imp, axis=0, keepdims=True)
        first = jnp.min(jnp.where(imp == best, sid_f, float(n_sel)), axis=0, keepdims=True)
        hit = sid_f == first
        sel = jnp.where(hit, 1.0, sel)
        imp = jnp.where(hit, -jnp.inf, imp)
    bias_t = jnp.where(sel > 0.5, 0.0, NEG_BIG).astype(BF16)
    q_t = [q_ref[0, :, r * hd:(r + 1) * hd].astype(F32).T.astype(BF16) for r in range(R)]
    q_aug_t = jnp.concatenate([jnp.concatenate(q_t, axis=1),
                               jnp.concatenate([bias_t] * R, axis=1)], axis=0)
    t_lanes = s0 + lax.broadcasted_iota(jnp.int32, (1, rows), 1) % QB

    def sel_tile(j, carry, causal):
        m_prev, l_prev, acc = carry
        base = pl.multiple_of(j * SEL_TILE, SEL_TILE)
        s = _dot(kaug_ref[pl.ds(base, SEL_TILE), :], q_aug_t)
        if causal:
            kpos = base + lax.broadcasted_iota(jnp.int32, (SEL_TILE, 1), 0)
            s = jnp.where(kpos <= t_lanes, s, NEG_BIG)
        m_new = jnp.maximum(m_prev, jnp.max(s, axis=0, keepdims=True))
        corr = jnp.exp(m_prev - m_new)
        p = jnp.exp(s - m_new)
        l_new = l_prev * corr + jnp.sum(p, axis=0, keepdims=True)
        acc = acc * corr + _dot(vst_ref[0, 0, :, pl.ds(base, SEL_TILE)], p.astype(BF16))
        return m_new, l_new, acc

    n_past = s0 // SEL_TILE
    init = (jnp.full((1, rows), NEG_BIG, F32), jnp.zeros((1, rows), F32), jnp.zeros((hd, rows), F32))
    carry = lax.fori_loop(0, n_past, functools.partial(sel_tile, causal=False), init)
    _, l_sel, acc_sel = sel_tile(n_past, carry, True)
    o_slc_t = acc_sel * (1.0 / l_sel)
    o_slc = jnp.concatenate([o_slc_t[:, r * QB:(r + 1) * QB].T for r in range(R)], axis=0)

    start = pl.multiple_of(s0, QB)
    col = lax.broadcasted_iota(jnp.int32, (1, n_win), 1)
    win_bias = band_ref[...] + jnp.where(col + s0 >= WINDOW, 0.0, NEG_BIG)
    sw = _dot_nt(q4, kw_ref[0, 0, pl.ds(start, n_win), :])
    sw = jnp.concatenate([sw[r * QB:(r + 1) * QB] + win_bias for r in range(R)], axis=0)
    pw = jnp.exp(sw - jnp.max(sw, axis=-1, keepdims=True))
    lw = jnp.sum(pw, axis=-1, keepdims=True)
    o_win = _dot(pw.astype(BF16), vw_ref[0, 0, pl.ds(start, n_win), :]) * (1.0 / lw)

    gates = g_ref[0]
    for r in range(R):
        sl = slice(r * QB, (r + 1) * QB)
        c = r * N_GATES
        out = (gates[:, c:c + 1] * o_cmp[sl] + gates[:, c + 1:c + 2] * o_slc[sl]
               + gates[:, c + 2:c + 3] * o_win[sl])
        o_ref[0, :, r * hd:(r + 1) * hd] = out.astype(o_ref.dtype)


def _nsa_attention(q, gates, kv_cmp, kv_sel, kv_win):
    bsz, seq, _ = q.shape
    G, hd = NSA_KV_HEADS, NSA_HEAD_DIM
    n_cmp = kv_cmp.shape[2]
    gw = NSA_REP * hd
    n_win = WINDOW + Q_BLOCK
    v_sel_t = jnp.swapaxes(kv_sel[:, G:2 * G], 2, 3)

    def slot(k):
        return lambda b, g, i: (b, k * G + g, 0, 0)

    return pl.pallas_call(
        _nsa_kernel,
        grid=(bsz, G, seq // Q_BLOCK),
        in_specs=[pl.BlockSpec((1, Q_BLOCK, gw), lambda b, g, i: (b, i, g)),
                  pl.BlockSpec((1, Q_BLOCK, LANES), lambda b, g, i: (b, i, g)),
                  pl.BlockSpec((1, 1, n_cmp, hd), slot(0)),
                  pl.BlockSpec((1, 1, n_cmp, hd), slot(1)),
                  pl.BlockSpec((1, 1, seq, hd), slot(0)),
                  pl.BlockSpec((1, 1, hd, seq), lambda b, g, i: (b, g, 0, 0)),
                  pl.BlockSpec((1, 1, WINDOW + seq, hd), slot(0)),
                  pl.BlockSpec((1, 1, WINDOW + seq, hd), slot(1))],
        out_specs=pl.BlockSpec((1, Q_BLOCK, gw), lambda b, g, i: (b, i, g)),
        out_shape=jax.ShapeDtypeStruct(q.shape, BF16),
        scratch_shapes=[pltpu.VMEM((seq, hd + seq // SEL_LEN), BF16),
                        pltpu.VMEM((Q_BLOCK, n_win), F32),
                        pltpu.VMEM((seq // SEL_LEN, n_cmp), BF16)],
        compiler_params=_params("arbitrary", "arbitrary", "arbitrary"),
        name="nsa_attention",
    )(q, gates, kv_cmp, kv_cmp, kv_sel, v_sel_t, kv_win, kv_win)


def _s5_layer(x, xin, bsz, seq, w_in, tables, d_skip, w_glu, w_out, g, b):
    m = bsz * seq
    u = _mm(xin, w_in, F32, tm=1024, tn=1024)
    y = _s5_core(u.reshape(bsz, seq, D_MODEL), tables, d_skip)
    z = _mm_glu(y.reshape(m, D_MODEL), w_glu)
    return _mm_res_ln(z, w_out, x, g, b)


def _nsa_layer(x, xb, bsz, seq, kv_cmp, kv_sel, kv_win, w_q, w_gate, w_o, g, b):
    m = bsz * seq
    q = _mm(xb, w_q, BF16, tm=1024, tn=1024, scale=NSA_HEAD_DIM ** -0.5)
    gates = _mm(xb, w_gate, F32, tm=1024, tn=512, act="sigmoid")
    o = _nsa_attention(q.reshape(bsz, seq, -1), gates.reshape(bsz, seq, -1), kv_cmp, kv_sel, kv_win)
    return _mm_res_ln(o.reshape(m, -1), w_o, x, g, b)


def _gate_weights(w_qg):
    n_q = NSA_HEADS * NSA_HEAD_DIM
    per = NSA_REP * N_GATES
    wg = w_qg[:, n_q:].reshape(-1, NSA_KV_HEADS, per)
    wg = jnp.pad(wg, ((0, 0), (0, 0), (0, LANES - per)))
    return wg.reshape(-1, NSA_KV_HEADS * LANES).astype(BF16)


def kernel(x, mem, s5_w_in, s5_a_re, s5_a_im, s5_log_dt, s5_b_re, s5_b_im, s5_c_re, s5_c_im, s5_d, s5_w_glu, s5_w_out, kv_w, cmp_pos_k, cmp_w1_k, cmp_w2_k, cmp_pos_v, cmp_w1_v, cmp_w2_v, nsa_w_qg, nsa_w_o, mem_w_q, mem_w_kv, mem_w_o, mlp_w_up, mlp_w_down, ln_g, ln_b):
    bsz, seq, d = x.shape
    m = bsz * seq
    bf = lambda w: w.astype(BF16)
    xf = x.reshape(m, d)
    xin = xf
    memf = mem.reshape(-1, d)
    kv_cmp = kv_sel = kv_win = None
    for layer in range(DEPTH):
        if layer < N_A_LAYERS:
            i = layer
            tables = _s5_tables(s5_a_re[i], s5_a_im[i], s5_log_dt[i], s5_b_re[i], s5_b_im[i],
                                s5_c_re[i], s5_c_im[i], seq // S5_CHUNK)
            xf, xin = _s5_layer(xf, xin, bsz, seq, bf(s5_w_in[i]), tables, s5_d[i], bf(s5_w_glu[i]),
                                bf(s5_w_out[i]), ln_g[layer, 0], ln_b[layer, 0])
        else:
            i = layer - N_A_LAYERS
            if xin.dtype != BF16:
                xin = xin.astype(BF16)
            if i == 0:
                n_cmp_cols = 2 * NSA_KV_HEADS * NSA_HEAD_DIM
                x3 = xin.reshape(bsz, seq, d)
                kv_c = _kv_proj(x3, bf(kv_w[:, :n_cmp_cols]), F32)
                kv_sel = _kv_proj(x3, bf(kv_w[:, n_cmp_cols:]), BF16)
                kv_win = jnp.pad(kv_sel[:, 2 * NSA_KV_HEADS:], ((0, 0), (0, 0), (WINDOW, 0), (0, 0)))
                half = CMP_STRIDE * NSA_HEAD_DIM
                pos = jnp.stack([cmp_pos_k.reshape(2, half), cmp_pos_v.reshape(2, half)])
                kv_cmp = _compress(kv_c, pos, jnp.stack([bf(cmp_w1_k), bf(cmp_w1_v)]),
                                   jnp.stack([bf(cmp_w2_k), bf(cmp_w2_v)]))
            n_q = NSA_HEADS * NSA_HEAD_DIM
            xf, xin = _nsa_layer(xf, xin, bsz, seq, kv_cmp, kv_sel, kv_win, bf(nsa_w_qg[i][:, :n_q]),
                                 _gate_weights(nsa_w_qg[i]), bf(nsa_w_o[i]),
                                 ln_g[layer, 0], ln_b[layer, 0])
        kvm = _mm(memf, bf(mem_w_kv[layer]), BF16, tm=512, tn=1024)
        o = _mem_attn(xin.reshape(bsz, seq, d), bf(mem_w_q[layer]), kvm.reshape(bsz, -1, 2 * d))
        xf, xin = _mm_res_ln(o.reshape(m, d), bf(mem_w_o[layer]), xf, ln_g[layer, 1], ln_b[layer, 1])
        xf, xin = _mlp_ln(xin, xf, bf(mlp_w_up[layer]), bf(mlp_w_down[layer]),
                          ln_g[layer, 2], ln_b[layer, 2])
    return xf.reshape(bsz, seq, d)
```

```python
import functools
import math

import jax
import jax.numpy as jnp
import numpy as np
from jax import lax
from jax.experimental import pallas as pl
from jax.experimental.pallas import tpu as pltpu

F32 = jnp.float32
BF16 = jnp.bfloat16

D_MODEL = 2048
DEPTH = 2
N_A_LAYERS = DEPTH // 2

S5_GROUP = 16
S5_GROUPS = D_MODEL // S5_GROUP
S5_STATE = 64
S5_CHUNK = 16
S5_ROW = S5_CHUNK * S5_GROUP
S5_GROUP_BLOCK = 8

NSA_HEADS = 16
NSA_HEAD_DIM = D_MODEL // NSA_HEADS
NSA_KV_HEADS = 4
NSA_REP = NSA_HEADS // NSA_KV_HEADS
CMP_LEN = 32
CMP_STRIDE = 16
SEL_LEN = 64
SEL_TOPN = 16
WINDOW = 512
Q_BLOCK = 512
N_GATES = 3
N_KV_SLOTS = 6
SEL_TILE = 1024

MEM_HEADS = 4
MEM_HEAD_DIM = D_MODEL // MEM_HEADS

D_FF = 4 * D_MODEL

DN_ALPHA = float((2 * DEPTH) ** 0.25)
LN_EPS = 1e-5
NEG_BIG = -1e30
SEL_FORCE = 1e9

LANES = 128
VMEM_LIMIT = 56 * 1024 * 1024


def _params(*semantics):
    return pltpu.CompilerParams(dimension_semantics=semantics, vmem_limit_bytes=VMEM_LIMIT)


def _layer_norm(y, g, b):
    mu = jnp.mean(y, axis=-1, keepdims=True)
    yc = y - mu
    var = jnp.mean(yc * yc, axis=-1, keepdims=True)
    return yc * lax.rsqrt(var + LN_EPS) * g + b


def _dot(a, b):
    return jnp.dot(a, b, preferred_element_type=F32)


def _dot_nt(a, b):
    return lax.dot_general(a, b, (((1,), (1,)), ((), ())), preferred_element_type=F32)


def _mm_kernel(x_ref, w_ref, o_ref, *, scale, act):
    acc = _dot(x_ref[...].astype(BF16), w_ref[...])
    if scale is not None:
        acc = acc * scale
    if act == "sigmoid":
        acc = jax.nn.sigmoid(acc)
    o_ref[...] = acc.astype(o_ref.dtype)


def _mm(x, w, out_dtype, *, tm, tn, scale=None, act=None):
    m, k = x.shape
    n = w.shape[1]
    tm, tn = min(tm, m), min(tn, n)
    return pl.pallas_call(
        functools.partial(_mm_kernel, scale=scale, act=act),
        grid=(m // tm, n // tn),
        in_specs=[pl.BlockSpec((tm, k), lambda i, j: (i, 0)),
                  pl.BlockSpec((k, tn), lambda i, j: (0, j))],
        out_specs=pl.BlockSpec((tm, tn), lambda i, j: (i, j)),
        out_shape=jax.ShapeDtypeStruct((m, n), out_dtype),
        compiler_params=_params("parallel", "arbitrary"),
        name="mm",
    )(x, w)


def _glu_kernel(y_ref, wv_ref, wg_ref, o_ref):
    y = y_ref[...].astype(BF16)
    val = _dot(y, wv_ref[...])
    gate = _dot(y, wg_ref[...])
    o_ref[...] = (val * jax.nn.sigmoid(gate)).astype(o_ref.dtype)


def _mm_glu(y, w_glu, *, tm=1024, tn=512):
    m, k = y.shape
    n = w_glu.shape[1] // 2
    tm = min(tm, m)
    nb = n // tn
    return pl.pallas_call(
        _glu_kernel,
        grid=(m // tm, nb),
        in_specs=[pl.BlockSpec((tm, k), lambda i, j: (i, 0)),
                  pl.BlockSpec((k, tn), lambda i, j: (0, j)),
                  pl.BlockSpec((k, tn), lambda i, j: (0, j + nb))],
        out_specs=pl.BlockSpec((tm, tn), lambda i, j: (i, j)),
        out_shape=jax.ShapeDtypeStruct((m, n), BF16),
        compiler_params=_params("parallel", "arbitrary"),
        name="mm_glu",
    )(y, w_glu, w_glu)


def _res_ln_kernel(z_ref, w_ref, x_ref, g_ref, b_ref, o_ref, ob_ref):
    h = _dot(z_ref[...], w_ref[...])
    y = _layer_norm(DN_ALPHA * x_ref[...] + h, g_ref[...], b_ref[...])
    o_ref[...] = y
    ob_ref[...] = y.astype(BF16)


def _mm_res_ln(z, w, x, g, b, *, tm=512):
    m, k = z.shape
    d = w.shape[1]
    tm = min(tm, m)
    row = lambda i: (i, 0)
    fixed = lambda i: (0, 0)
    return pl.pallas_call(
        _res_ln_kernel,
        grid=(m // tm,),
        in_specs=[pl.BlockSpec((tm, k), row), pl.BlockSpec((k, d), fixed),
                  pl.BlockSpec((tm, d), row), pl.BlockSpec((1, d), fixed),
                  pl.BlockSpec((1, d), fixed)],
        out_specs=[pl.BlockSpec((tm, d), row), pl.BlockSpec((tm, d), row)],
        out_shape=[jax.ShapeDtypeStruct((m, d), F32), jax.ShapeDtypeStruct((m, d), BF16)],
        compiler_params=_params("parallel"),
        name="proj_postnorm",
    )(z, w, x, g.reshape(1, d), b.reshape(1, d))


def _mlp_kernel(xb_ref, x_ref, wu_ref, wd_ref, g_ref, b_ref, o_ref, ob_ref, acc_ref):
    j = pl.program_id(1)
    h = jnp.maximum(_dot(xb_ref[...], wu_ref[...]), 0.0)
    part = _dot((h * h).astype(BF16), wd_ref[...])

    @pl.when(j == 0)
    def _():
        acc_ref[...] = part

    @pl.when(j > 0)
    def _():
        acc_ref[...] += part

    @pl.when(j == pl.num_programs(1) - 1)
    def _():
        y = _layer_norm(DN_ALPHA * x_ref[...] + acc_ref[...], g_ref[...], b_ref[...])
        o_ref[...] = y
        ob_ref[...] = y.astype(BF16)


def _mlp_ln(xb, x, w_up, w_down, g, b, *, tm=512, tf=1024):
    m, d = x.shape
    ff = w_up.shape[1]
    tm = min(tm, m)
    row = lambda i, j: (i, 0)
    fixed = lambda i, j: (0, 0)
    return pl.pallas_call(
        _mlp_kernel,
        grid=(m // tm, ff // tf),
        in_specs=[pl.BlockSpec((tm, d), row), pl.BlockSpec((tm, d), row),
                  pl.BlockSpec((d, tf), lambda i, j: (0, j)),
                  pl.BlockSpec((tf, d), lambda i, j: (j, 0)),
                  pl.BlockSpec((1, d), fixed), pl.BlockSpec((1, d), fixed)],
        out_specs=[pl.BlockSpec((tm, d), row), pl.BlockSpec((tm, d), row)],
        out_shape=[jax.ShapeDtypeStruct((m, d), F32), jax.ShapeDtypeStruct((m, d), BF16)],
        scratch_shapes=[pltpu.VMEM((tm, d), F32)],
        compiler_params=_params("parallel", "arbitrary"),
        name="mlp_postnorm",
    )(xb, x, w_up, w_down, g.reshape(1, d), b.reshape(1, d))


def _mem_attn_kernel(x_ref, wq_ref, kv_ref, o_ref):
    q = _dot(x_ref[0], wq_ref[...]).astype(BF16)
    scale = MEM_HEAD_DIM ** -0.5
    for h in range(MEM_HEADS):
        lo = h * MEM_HEAD_DIM
        k = kv_ref[0, :, lo:lo + MEM_HEAD_DIM]
        v = kv_ref[0, :, D_MODEL + lo:D_MODEL + lo + MEM_HEAD_DIM]
        s = _dot_nt(q[:, lo:lo + MEM_HEAD_DIM], k) * scale
        p = jnp.exp(s - jnp.max(s, axis=-1, keepdims=True))
        p = p / jnp.sum(p, axis=-1, keepdims=True)
        o_ref[0, :, lo:lo + MEM_HEAD_DIM] = _dot(p.astype(BF16), v).astype(o_ref.dtype)


def _mem_attn(xb, wq, kvm, *, tm=512):
    bsz, seq, d = xb.shape
    n_mem = kvm.shape[1]
    tm = min(tm, seq)
    return pl.pallas_call(
        _mem_attn_kernel,
        grid=(bsz, seq // tm),
        in_specs=[pl.BlockSpec((1, tm, d), lambda b, i: (b, i, 0)),
                  pl.BlockSpec((d, d), lambda b, i: (0, 0)),
                  pl.BlockSpec((1, n_mem, 2 * d), lambda b, i: (b, 0, 0))],
        out_specs=pl.BlockSpec((1, tm, d), lambda b, i: (b, i, 0)),
        out_shape=jax.ShapeDtypeStruct((bsz, seq, d), BF16),
        compiler_params=_params("parallel", "arbitrary"),
        name="mem_attn",
    )(xb, wq, kvm)


def _s5_tables(a_re, a_im, log_dt, b_re, b_im, c_re, c_im, n_chunks):
    hi = lax.Precision.HIGHEST
    G, P, GS, T = S5_GROUPS, S5_STATE, S5_GROUP, S5_CHUNK
    dt = jnp.exp(log_dt.astype(F32))[:, None]
    lr, li = a_re.astype(F32), a_im.astype(F32)
    mag = jnp.exp(lr * dt)
    ab_re, ab_im = mag * jnp.cos(li * dt), mag * jnp.sin(li * dt)
    den = lr * lr + li * li
    nr, ni = ab_re - 1.0, ab_im
    f_re = (nr * lr + ni * li) / den
    f_im = (ni * lr - nr * li) / den
    br, bi = b_re.astype(F32), b_im.astype(F32)
    bb_re = f_re[..., None] * br - f_im[..., None] * bi
    bb_im = f_re[..., None] * bi + f_im[..., None] * br

    def apow(n):
        n = jnp.asarray(n, F32)[:, None, None]
        m = jnp.exp(lr * dt * n)
        return m * jnp.cos(li * dt * n), m * jnp.sin(li * dt * n)

    pw_re, pw_im = apow(np.arange(T + 1))
    abb_re = pw_re[..., None] * bb_re - pw_im[..., None] * bb_im
    abb_im = pw_re[..., None] * bb_im + pw_im[..., None] * bb_re

    rev = np.arange(T - 1, -1, -1)
    w_inc = jnp.concatenate([abb_re[rev], abb_im[rev]], axis=2)
    w_inc = w_inc.transpose(1, 0, 3, 2).reshape(G, T * GS, 2 * P)

    cr, ci = c_re.astype(F32), c_im.astype(F32)
    kern = (jnp.einsum("ghp,ngpk->nghk", cr, abb_re[:T], precision=hi)
            - jnp.einsum("ghp,ngpk->nghk", ci, abb_im[:T], precision=hi))
    tau = np.arange(T)[None, :] - np.arange(T)[:, None]
    toep = jnp.where((tau >= 0)[:, :, None, None, None], kern[np.clip(tau, 0, T - 1)], 0.0)
    w_toep = toep.transpose(2, 0, 4, 1, 3).reshape(G, T * GS, T * GS)

    pr, pi = pw_re[1:], pw_im[1:]
    car_re = cr[None] * pr[:, :, None, :] - ci[None] * pi[:, :, None, :]
    car_im = cr[None] * pi[:, :, None, :] + ci[None] * pr[:, :, None, :]
    w_car = jnp.concatenate([car_re, -car_im], axis=3)
    w_car = w_car.transpose(1, 3, 0, 2).reshape(G, 2 * P, T * GS)

    n_steps = max(1, int(math.ceil(math.log2(n_chunks))))
    sr, si = apow(T * (2 ** np.arange(n_steps)))
    apow_a = jnp.concatenate([sr, sr], axis=2).transpose(1, 0, 2)
    apow_b = jnp.concatenate([-si, si], axis=2).transpose(1, 0, 2)
    return w_inc.astype(BF16), w_toep.astype(BF16), w_car.astype(BF16), apow_a, apow_b


def _s5_kernel(u_ref, winc_ref, wtoep_ref, wcar_ref, pa_ref, pb_ref, d_ref, o_ref, *, n_steps):
    T, GS, GB = S5_CHUNK, S5_GROUP, S5_GROUP_BLOCK
    n_chunks = u_ref.shape[1] // T
    rows = lax.broadcasted_iota(jnp.int32, (n_chunks, 2 * S5_STATE), 0)
    xs = [u_ref[0, pl.ds(j, n_chunks, stride=T), :] for j in range(T)]
    ys = []
    for g in range(GB):
        ub = jnp.concatenate([x[:, g * GS:(g + 1) * GS] for x in xs], axis=1).astype(BF16)
        s = _dot(ub, winc_ref[g])
        for k in range(n_steps):
            d = 1 << k
            sh = jnp.where(rows >= d, pltpu.roll(s, d, 0), 0.0)
            sw = pltpu.roll(sh, S5_STATE, 1)
            s = s + sh * pa_ref[g, k:k + 1, :] + sw * pb_ref[g, k:k + 1, :]
        s_prev = jnp.where(rows >= 1, pltpu.roll(s, 1, 0), 0.0)
        ys.append(_dot(ub, wtoep_ref[g]) + _dot(s_prev.astype(BF16), wcar_ref[g]))
    for t in range(T):
        y = jnp.concatenate([yg[:, t * GS:(t + 1) * GS] for yg in ys], axis=1)
        y = y + d_ref[...] * xs[t]
        o_ref[0, pl.ds(t, n_chunks, stride=T), :] = jax.nn.gelu(y)


def _s5_core(u, tables, d_skip):
    bsz, seq, d = u.shape
    w_inc, w_toep, w_car, apow_a, apow_b = tables
    n_steps = apow_a.shape[1]
    gb = S5_GROUP_BLOCK
    blk = lambda b, i: (b, 0, i)
    wblk = lambda b, i: (i, 0, 0)
    return pl.pallas_call(
        functools.partial(_s5_kernel, n_steps=n_steps),
        grid=(bsz, S5_GROUPS // gb),
        in_specs=[pl.BlockSpec((1, seq, LANES), blk),
                  pl.BlockSpec((gb,) + w_inc.shape[1:], wblk),
                  pl.BlockSpec((gb,) + w_toep.shape[1:], wblk),
                  pl.BlockSpec((gb,) + w_car.shape[1:], wblk),
                  pl.BlockSpec((gb,) + apow_a.shape[1:], wblk),
                  pl.BlockSpec((gb,) + apow_b.shape[1:], wblk),
                  pl.BlockSpec((1, LANES), lambda b, i: (0, i))],
        out_specs=pl.BlockSpec((1, seq, LANES), blk),
        out_shape=jax.ShapeDtypeStruct(u.shape, F32),
        compiler_params=_params("parallel", "arbitrary"),
        name="s5_scan",
    )(u, w_inc, w_toep, w_car, apow_a, apow_b, d_skip.astype(F32).reshape(1, d))


def _kv_proj_kernel(x_ref, w_ref, o_ref):
    acc = _dot(x_ref[0], w_ref[...])
    for s in range(o_ref.shape[1]):
        o_ref[0, s] = acc[:, s * NSA_HEAD_DIM:(s + 1) * NSA_HEAD_DIM].astype(o_ref.dtype)


def _kv_proj(xb, w, out_dtype, *, tm=1024, slabs=8):
    bsz, seq, d = xb.shape
    n_slabs = w.shape[1] // NSA_HEAD_DIM
    tm = min(tm, seq)
    return pl.pallas_call(
        _kv_proj_kernel,
        grid=(bsz, seq // tm, n_slabs // slabs),
        in_specs=[pl.BlockSpec((1, tm, d), lambda b, i, j: (b, i, 0)),
                  pl.BlockSpec((d, slabs * NSA_HEAD_DIM), lambda b, i, j: (0, j))],
        out_specs=pl.BlockSpec((1, slabs, tm, NSA_HEAD_DIM), lambda b, i, j: (b, j, i, 0)),
        out_shape=jax.ShapeDtypeStruct((bsz, n_slabs, seq, NSA_HEAD_DIM), out_dtype),
        compiler_params=_params("parallel", "parallel", "arbitrary"),
        name="kv_proj",
    )(xb, w)


def _compress_kernel(t_ref, pos_ref, w1_ref, w2_ref, o_ref):
    t = t_ref[0, 0]
    n = t.shape[0]
    half = CMP_STRIDE * NSA_HEAD_DIM
    first = _dot((t + pos_ref[0, 0:1, :]).astype(BF16), w1_ref[0, :half, :])
    second = _dot((t + pos_ref[0, 1:2, :]).astype(BF16), w1_ref[0, half:, :])
    pre = first + pltpu.roll(second, n - 1, 0)
    out = _dot(jax.nn.gelu(pre).astype(BF16), w2_ref[0])
    rows = lax.broadcasted_iota(jnp.int32, out.shape, 0)
    o_ref[0, 0] = jnp.where(rows < n - 1, out, 0.0).astype(o_ref.dtype)


def _compress(kv_cmp, pos, w1, w2):
    bsz, n_slots, seq, hd = kv_cmp.shape
    n16 = seq // CMP_STRIDE
    t16 = kv_cmp.reshape(bsz, n_slots, n16, CMP_STRIDE * hd)
    kvh = NSA_KV_HEADS
    return pl.pallas_call(
        _compress_kernel,
        grid=(bsz, n_slots),
        in_specs=[pl.BlockSpec((1, 1, n16, CMP_STRIDE * hd), lambda b, s: (b, s, 0, 0)),
                  pl.BlockSpec((1, 2, CMP_STRIDE * hd), lambda b, s: (s // kvh, 0, 0)),
                  pl.BlockSpec((1, CMP_LEN * hd, hd), lambda b, s: (s // kvh, 0, 0)),
                  pl.BlockSpec((1, hd, hd), lambda b, s: (s // kvh, 0, 0))],
        out_specs=pl.BlockSpec((1, 1, n16, hd), lambda b, s: (b, s, 0, 0)),
        out_shape=jax.ShapeDtypeStruct((bsz, n_slots, n16, hd), BF16),
        compiler_params=_params("parallel", "arbitrary"),
        name="nsa_compress",
    )(t16, pos, w1, w2)


def _nsa_kernel(q_ref, g_ref, kc_ref, vc_ref, ks_ref, vst_ref, kw_ref, vw_ref, o_ref,
                kaug_ref, band_ref, selmap_ref):
    qb = pl.program_id(2)
    s0 = qb * Q_BLOCK
    R, hd, QB = NSA_REP, NSA_HEAD_DIM, Q_BLOCK
    rows = R * QB
    n_cmp = kc_ref.shape[2]
    seq = ks_ref.shape[2]
    n_sel = seq // SEL_LEN
    n_win = band_ref.shape[1]

    @pl.when(qb == 0)
    def _():
        lane = lax.broadcasted_iota(jnp.int32, (SEL_TILE, n_sel), 1)
        blk = lax.broadcasted_iota(jnp.int32, (SEL_TILE, n_sel), 0) // SEL_LEN

        def fill(j, _):
            base = pl.multiple_of(j * SEL_TILE, SEL_TILE)
            onehot = jnp.where(lane == blk + j * (SEL_TILE // SEL_LEN), 1.0, 0.0).astype(BF16)
            kaug_ref[pl.ds(base, SEL_TILE), :] = jnp.concatenate(
                [ks_ref[0, 0, pl.ds(base, SEL_TILE), :], onehot], axis=1)
            return 0

        lax.fori_loop(0, seq // SEL_TILE, fill, 0)
        qi = lax.broadcasted_iota(jnp.int32, (QB, n_win), 0)
        col = lax.broadcasted_iota(jnp.int32, (QB, n_win), 1)
        band_ref[...] = jnp.where((col > qi) & (col <= qi + WINDOW), 0.0, NEG_BIG)
        sel_s = lax.broadcasted_iota(jnp.int32, (n_sel, n_cmp), 0) * SEL_LEN
        cmp_c = lax.broadcasted_iota(jnp.int32, (n_sel, n_cmp), 1) * CMP_STRIDE
        selmap_ref[...] = jnp.where((cmp_c < sel_s + SEL_LEN) & (cmp_c + CMP_LEN > sel_s),
                                    1.0, 0.0).astype(BF16)

    q4 = jnp.concatenate([q_ref[0, :, r * hd:(r + 1) * hd] for r in range(R)], axis=0)
    t_rows = s0 + lax.broadcasted_iota(jnp.int32, (rows, 1), 0) % QB

    cmp_end =lax.broadcasted_iota(jnp.int32, (1, n_cmp), 1) * CMP_STRIDE + (CMP_LEN - 1)
    sc = jnp.where(cmp_end <= t_rows, _dot_nt(q4, kc_ref[0, 0]), NEG_BIG)
    mc = jnp.max(sc, axis=-1, keepdims=True)
    pc = jnp.exp(sc - mc)
    lc = jnp.sum(pc, axis=-1, keepdims=True)
    pc = pc * jnp.where(mc > 0.5 * NEG_BIG, 1.0 / jnp.maximum(lc, 1e-30), 0.0)
    o_cmp = _dot(pc.astype(BF16), vc_ref[0, 0])

    pc_sum = pc[0:QB]
    for r in range(1, R):
        pc_sum = pc_sum + pc[r * QB:(r + 1) * QB]
    pc_hi = pc_sum.astype(BF16)
    pc_lo = (pc_sum - pc_hi.astype(F32)).astype(BF16)
    sel_map_t = selmap_ref[...]
    imp =_dot_nt(sel_map_t, pc_hi) + _dot_nt(sel_map_t, pc_lo)

    sid = lax.broadcasted_iota(jnp.int32, (n_sel, QB), 0)
    sid_f = sid.astype(F32)
    tq = s0 + lax.broadcasted_iota(jnp.int32, (n_sel, QB), 1)
    cur = tq // SEL_LEN
    forced = (sid == 0) | (sid == cur) | (sid == cur - 1)
    imp = jnp.where(forced, SEL_FORCE, jnp.where(sid * SEL_LEN <= tq, imp, -SEL_FORCE))

    top_n = min(SEL_TOPN, n_sel)
    sel = jnp.zeros((n_sel, QB), F32)
    for _ in range(top_n):
        best = jnp.max(imp, axis=0, keepdims=True)
        first = jnp.min(jnp.where(imp == best, sid_f, float(n_sel)), axis=0, keepdims=True)
        hit = sid_f == first
        sel = jnp.where(hit, 1.0, sel)
        imp = jnp.where(hit, -jnp.inf, imp)
    bias_t = jnp.where(sel > 0.5, 0.0, NEG_BIG).astype(BF16)
    q_t = [q_ref[0, :, r * hd:(r + 1) * hd].astype(F32).T.astype(BF16) for r in range(R)]
    q_aug_t = jnp.concatenate([jnp.concatenate(q_t, axis=1),
                               jnp.concatenate([bias_t] * R, axis=1)], axis=0)
    t_lanes = s0 + lax.broadcasted_iota(jnp.int32, (1, rows), 1) % QB

    def sel_tile(j, carry, causal):
        m_prev, l_prev, acc = carry
        base = pl.multiple_of(j * SEL_TILE, SEL_TILE)
        s = _dot(kaug_ref[pl.ds(base, SEL_TILE), :], q_aug_t)
        if causal:
            kpos = base + lax.broadcasted_iota(jnp.int32, (SEL_TILE, 1), 0)
            s = jnp.where(kpos <= t_lanes, s, NEG_BIG)
        m_new = jnp.maximum(m_prev, jnp.max(s, axis=0, keepdims=True))
        corr = jnp.exp(m_prev - m_new)
        p = jnp.exp(s - m_new)
        l_new = l_prev * corr + jnp.sum(p, axis=0, keepdims=True)
        acc = acc * corr + _dot(vst_ref[0, 0, :, pl.ds(base, SEL_TILE)], p.astype(BF16))
        return m_new, l_new, acc

    n_past = s0 // SEL_TILE
    init = (jnp.full((1, rows), NEG_BIG, F32), jnp.zeros((1, rows), F32), jnp.zeros((hd, rows), F32))
    carry = lax.fori_loop(0, n_past, functools.partial(sel_tile, causal=False), init)
    _, l_sel, acc_sel = sel_tile(n_past, carry, True)
    o_slc_t = acc_sel * (1.0 / l_sel)
    o_slc = jnp.concatenate([o_slc_t[:, r * QB:(r + 1) * QB].T for r in range(R)], axis=0)

    start = pl.multiple_of(s0, QB)
    col = lax.broadcasted_iota(jnp.int32, (1, n_win), 1)
    win_bias = band_ref[...] + jnp.where(col + s0 >= WINDOW, 0.0, NEG_BIG)
    sw = _dot_nt(q4, kw_ref[0, 0, pl.ds(start, n_win), :])
    sw = jnp.concatenate([sw[r * QB:(r + 1) * QB] + win_bias for r in range(R)], axis=0)
    pw = jnp.exp(sw - jnp.max(sw, axis=-1, keepdims=True))
    lw = jnp.sum(pw, axis=-1, keepdims=True)
    o_win = _dot(pw.astype(BF16), vw_ref[0, 0, pl.ds(start, n_win), :]) * (1.0 / lw)

    gates = g_ref[0]
    for r in range(R):
        sl = slice(r * QB, (r + 1) * QB)
        c = r * N_GATES
        out = (gates[:, c:c + 1] * o_cmp[sl] + gates[:, c + 1:c + 2] * o_slc[sl]
               + gates[:, c + 2:c + 3] * o_win[sl])
        o_ref[0, :, r * hd:(r + 1) * hd] = out.astype(o_ref.dtype)


def _nsa_attention(q, gates, kv_cmp, kv_sel, kv_win):
    bsz, seq, _ = q.shape
    G, hd = NSA_KV_HEADS, NSA_HEAD_DIM
    n_cmp = kv_cmp.shape[2]
    gw = NSA_REP * hd
    n_win = WINDOW + Q_BLOCK
    v_sel_t = jnp.swapaxes(kv_sel[:, G:2 * G], 2, 3)

    def slot(k):
        return lambda b, g, i: (b, k * G + g, 0, 0)

    return pl.pallas_call(
        _nsa_kernel,
        grid=(bsz, G, seq // Q_BLOCK),
        in_specs=[pl.BlockSpec((1, Q_BLOCK, gw), lambda b, g, i: (b, i, g)),
                  pl.BlockSpec((1, Q_BLOCK, LANES), lambda b, g, i: (b, i, g)),
                  pl.BlockSpec((1, 1, n_cmp, hd), slot(0)),
                  pl.BlockSpec((1, 1, n_cmp, hd), slot(1)),
                  pl.BlockSpec((1, 1, seq, hd), slot(0)),
                  pl.BlockSpec((1, 1, hd, seq), lambda b, g, i: (b, g, 0, 0)),
                  pl.BlockSpec((1, 1, WINDOW + seq, hd), slot(0)),
                  pl.BlockSpec((1, 1, WINDOW + seq, hd), slot(1))],
        out_specs=pl.BlockSpec((1, Q_BLOCK, gw), lambda b, g, i: (b, i, g)),
        out_shape=jax.ShapeDtypeStruct(q.shape, BF16),
        scratch_shapes=[pltpu.VMEM((seq, hd + seq // SEL_LEN), BF16),
                        pltpu.VMEM((Q_BLOCK, n_win), F32),
                        pltpu.VMEM((seq // SEL_LEN, n_cmp), BF16)],
        compiler_params=_params("arbitrary", "arbitrary", "arbitrary"),
        name="nsa_attention",
    )(q, gates, kv_cmp, kv_cmp, kv_sel, v_sel_t, kv_win, kv_win)


def _s5_layer(x, xin, bsz, seq, w_in, tables, d_skip, w_glu, w_out, g, b):
    m = bsz * seq
    u = _mm(xin, w_in, F32, tm=1024, tn=1024)
    y = _s5_core(u.reshape(bsz, seq, D_MODEL), tables, d_skip)
    z = _mm_glu(y.reshape(m, D_MODEL), w_glu)
    return _mm_res_ln(z, w_out, x, g, b)


def _nsa_layer(x, xb, bsz, seq, kv_cmp, kv_sel, kv_win, w_q, w_gate, w_o, g, b):
    m = bsz * seq
    q = _mm(xb, w_q, BF16, tm=1024, tn=1024, scale=NSA_HEAD_DIM ** -0.5)
    gates = _mm(xb, w_gate, F32, tm=1024, tn=512, act="sigmoid")
    o = _nsa_attention(q.reshape(bsz, seq, -1), gates.reshape(bsz, seq, -1), kv_cmp, kv_sel, kv_win)
    return _mm_res_ln(o.reshape(m, -1), w_o, x, g, b)


def _gate_weights(w_qg):
    n_q = NSA_HEADS * NSA_HEAD_DIM
    per = NSA_REP * N_GATES
    wg = w_qg[:, n_q:].reshape(-1, NSA_KV_HEADS, per)
    wg = jnp.pad(wg, ((0, 0), (0, 0), (0, LANES - per)))
    return wg.reshape(-1, NSA_KV_HEADS * LANES).astype(BF16)


def kernel(x, mem, s5_w_in, s5_a_re, s5_a_im, s5_log_dt, s5_b_re, s5_b_im, s5_c_re, s5_c_im, s5_d, s5_w_glu, s5_w_out, kv_w, cmp_pos_k, cmp_w1_k, cmp_w2_k, cmp_pos_v, cmp_w1_v, cmp_w2_v, nsa_w_qg, nsa_w_o, mem_w_q, mem_w_kv, mem_w_o, mlp_w_up, mlp_w_down, ln_g, ln_b):
    bsz, seq, d = x.shape
    m = bsz * seq
    bf = lambda w: w.astype(BF16)
    xf = x.reshape(m, d)
    xin = xf
    memf = mem.reshape(-1, d)
    kv_cmp = kv_sel = kv_win = None
    for layer in range(DEPTH):
        if layer < N_A_LAYERS:
            i = layer
            tables = _s5_tables(s5_a_re[i], s5_a_im[i], s5_log_dt[i], s5_b_re[i], s5_b_im[i],
                                s5_c_re[i], s5_c_im[i], seq // S5_CHUNK)
            xf, xin = _s5_layer(xf, xin, bsz, seq, bf(s5_w_in[i]), tables, s5_d[i], bf(s5_w_glu[i]),
                                bf(s5_w_out[i]), ln_g[layer, 0], ln_b[layer, 0])
        else:
            i = layer - N_A_LAYERS
            if xin.dtype != BF16:
                xin = xin.astype(BF16)
            if i == 0:
                n_cmp_cols = 2 * NSA_KV_HEADS * NSA_HEAD_DIM
                x3 = xin.reshape(bsz, seq, d)
                kv_c = _kv_proj(x3, bf(kv_w[:, :n_cmp_cols]), F32)
                kv_sel = _kv_proj(x3, bf(kv_w[:, n_cmp_cols:]), BF16)
                kv_win = jnp.pad(kv_sel[:, 2 * NSA_KV_HEADS:], ((0, 0), (0, 0), (WINDOW, 0), (0, 0)))
                half = CMP_STRIDE * NSA_HEAD_DIM
                pos = jnp.stack([cmp_pos_k.reshape(2, half), cmp_pos_v.reshape(2, half)])
                kv_cmp = _compress(kv_c, pos, jnp.stack([bf(cmp_w1_k), bf(cmp_w1_v)]),
                                   jnp.stack([bf(cmp_w2_k), bf(cmp_w2_v)]))
            n_q = NSA_HEADS * NSA_HEAD_DIM
            xf, xin = _nsa_layer(xf, xin, bsz, seq, kv_cmp, kv_sel, kv_win, bf(nsa_w_qg[i][:, :n_q]),
                                 _gate_weights(nsa_w_qg[i]), bf(nsa_w_o[i]),
                                 ln_g[layer, 0], ln_b[layer, 0])
        kvm = _mm(memf, bf(mem_w_kv[layer]), BF16, tm=512, tn=1024)
        o = _mem_attn(xin.reshape(bsz, seq, d), bf(mem_w_q[layer]), kvm.reshape(bsz, -1, 2 * d))
        xf, xin = _mm_res_ln(o.reshape(m, d), bf(mem_w_o[layer]), xf, ln_g[layer, 1], ln_b[layer, 1])
        xf, xin = _mlp_ln(xin, xf, bf(mlp_w_up[layer]), bf(mlp_w_down[layer]),
                          ln_g[layer, 2], ln_b[layer, 2])
    return xf.reshape(bsz, seq, d)
```

```python
import functools
import math

import jax
import jax.numpy as jnp
import numpy as np
from jax import lax
from jax.experimental import pallas as pl
from jax.experimental.pallas import tpu as pltpu

F32 = jnp.float32
BF16 = jnp.bfloat16

D_MODEL = 2048
DEPTH = 2
N_A_LAYERS = DEPTH // 2

S5_GROUP = 16
S5_GROUPS = D_MODEL // S5_GROUP
S5_STATE = 64
S5_CHUNK = 16
S5_ROW = S5_CHUNK * S5_GROUP
S5_GROUP_BLOCK = 8

NSA_HEADS = 16
NSA_HEAD_DIM = D_MODEL // NSA_HEADS
NSA_KV_HEADS = 4
NSA_REP = NSA_HEADS // NSA_KV_HEADS
CMP_LEN = 32
CMP_STRIDE = 16
SEL_LEN = 64
SEL_TOPN = 16
WINDOW = 512
Q_BLOCK = 512
N_GATES = 3
N_KV_SLOTS = 6
SEL_TILE = 1024
WIN_BLOCK = 128

MEM_HEADS = 4
MEM_HEAD_DIM = D_MODEL // MEM_HEADS

D_FF = 4 * D_MODEL

DN_ALPHA = float((2 * DEPTH) ** 0.25)
LN_EPS = 1e-5
NEG_BIG = -1e30
SEL_FORCE = 1e9

LANES = 128
VMEM_LIMIT = 56 * 1024 * 1024


def _params(*semantics):
    return pltpu.CompilerParams(dimension_semantics=semantics, vmem_limit_bytes=VMEM_LIMIT)


def _layer_norm(y, g, b):
    mu = jnp.mean(y, axis=-1, keepdims=True)
    yc = y - mu
    var = jnp.mean(yc * yc, axis=-1, keepdims=True)
    return yc * lax.rsqrt(var + LN_EPS) * g + b


def _dot(a, b):
    return jnp.dot(a, b, preferred_element_type=F32)


def _dot_nt(a, b):
    return lax.dot_general(a, b, (((1,), (1,)), ((), ())), preferred_element_type=F32)


def _mm_kernel(x_ref, w_ref, o_ref, *, scale, act):
    acc = _dot(x_ref[...].astype(BF16), w_ref[...])
    if scale is not None:
        acc = acc * scale
    if act == "sigmoid":
        acc = jax.nn.sigmoid(acc)
    o_ref[...] = acc.astype(o_ref.dtype)


def _mm(x, w, out_dtype, *, tm, tn, scale=None, act=None):
    m, k = x.shape
    n = w.shape[1]
    tm, tn = min(tm, m), min(tn, n)
    return pl.pallas_call(
        functools.partial(_mm_kernel, scale=scale, act=act),
        grid=(m // tm, n // tn),
        in_specs=[pl.BlockSpec((tm, k), lambda i, j: (i, 0)),
                  pl.BlockSpec((k, tn), lambda i, j: (0, j))],
        out_specs=pl.BlockSpec((tm, tn), lambda i, j: (i, j)),
        out_shape=jax.ShapeDtypeStruct((m, n), out_dtype),
        compiler_params=_params("parallel", "arbitrary"),
        name="mm",
    )(x, w)


def _glu_kernel(y_ref, wv_ref, wg_ref, o_ref):
    y = y_ref[...].astype(BF16)
    val = _dot(y, wv_ref[...])
    gate = _dot(y, wg_ref[...])
    o_ref[...] = (val * jax.nn.sigmoid(gate)).astype(o_ref.dtype)


def _mm_glu(y, w_glu, *, tm=1024, tn=512):
    m, k = y.shape
    n = w_glu.shape[1] // 2
    tm = min(tm, m)
    nb = n // tn
    return pl.pallas_call(
        _glu_kernel,
        grid=(m // tm, nb),
        in_specs=[pl.BlockSpec((tm, k), lambda i, j: (i, 0)),
                  pl.BlockSpec((k, tn), lambda i, j: (0, j)),
                  pl.BlockSpec((k, tn), lambda i, j: (0, j + nb))],
        out_specs=pl.BlockSpec((tm, tn), lambda i, j: (i, j)),
        out_shape=jax.ShapeDtypeStruct((m, n), BF16),
        compiler_params=_params("parallel", "arbitrary"),
        name="mm_glu",
    )(y, w_glu, w_glu)


def _res_ln_kernel(z_ref, w_ref, x_ref, g_ref, b_ref, o_ref, ob_ref):
    h = _dot(z_ref[...], w_ref[...])
    y = _layer_norm(DN_ALPHA * x_ref[...] + h, g_ref[...], b_ref[...])
    o_ref[...] = y
    ob_ref[...] = y.astype(BF16)


def _mm_res_ln(z, w, x, g, b, *, tm=512):
    m, k = z.shape
    d = w.shape[1]
    tm = min(tm, m)
    row = lambda i: (i, 0)
    fixed = lambda i: (0, 0)
    return pl.pallas_call(
        _res_ln_kernel,
        grid=(m // tm,),
        in_specs=[pl.BlockSpec((tm, k), row), pl.BlockSpec((k, d), fixed),
                  pl.BlockSpec((tm, d), row), pl.BlockSpec((1, d), fixed),
                  pl.BlockSpec((1, d), fixed)],
        out_specs=[pl.BlockSpec((tm, d), row), pl.BlockSpec((tm, d), row)],
        out_shape=[jax.ShapeDtypeStruct((m, d), F32), jax.ShapeDtypeStruct((m, d), BF16)],
        compiler_params=_params("parallel"),
        name="proj_postnorm",
    )(z, w, x, g.reshape(1, d), b.reshape(1, d))


def _mlp_kernel(xb_ref, x_ref, wu_ref, wd_ref, g_ref, b_ref, o_ref, ob_ref, acc_ref):
    j = pl.program_id(1)
    h = jnp.maximum(_dot(xb_ref[...], wu_ref[...]), 0.0)
    part = _dot((h * h).astype(BF16), wd_ref[...])

    @pl.when(j == 0)
    def _():
        acc_ref[...] = part

    @pl.when(j > 0)
    def _():
        acc_ref[...] += part

    @pl.when(j == pl.num_programs(1) - 1)
    def _():
        y = _layer_norm(DN_ALPHA * x_ref[...] + acc_ref[...], g_ref[...], b_ref[...])
        o_ref[...] = y
        ob_ref[...] = y.astype(BF16)


def _mlp_ln(xb, x, w_up, w_down, g, b, *, tm=512, tf=1024):
    m, d = x.shape
    ff = w_up.shape[1]
    tm = min(tm, m)
    row = lambda i, j: (i, 0)
    fixed = lambda i, j: (0, 0)
    return pl.pallas_call(
        _mlp_kernel,
        grid=(m // tm, ff // tf),
        in_specs=[pl.BlockSpec((tm, d), row), pl.BlockSpec((tm, d), row),
                  pl.BlockSpec((d, tf), lambda i, j: (0, j)),
                  pl.BlockSpec((tf, d), lambda i, j: (j, 0)),
                  pl.BlockSpec((1, d), fixed), pl.BlockSpec((1, d), fixed)],
        out_specs=[pl.BlockSpec((tm, d), row), pl.BlockSpec((tm, d), row)],
        out_shape=[jax.ShapeDtypeStruct((m, d), F32), jax.ShapeDtypeStruct((m, d), BF16)],
        scratch_shapes=[pltpu.VMEM((tm, d), F32)],
        compiler_params=_params("parallel", "arbitrary"),
        name="mlp_postnorm",
    )(xb, x, w_up, w_down, g.reshape(1, d), b.reshape(1, d))


def _mem_attn_kernel(x_ref, wq_ref, kv_ref, o_ref):
    q = _dot(x_ref[0], wq_ref[...]).astype(BF16)
    scale = MEM_HEAD_DIM ** -0.5
    for h in range(MEM_HEADS):
        lo = h * MEM_HEAD_DIM
        k = kv_ref[0, :, lo:lo + MEM_HEAD_DIM]
        v = kv_ref[0, :, D_MODEL + lo:D_MODEL + lo + MEM_HEAD_DIM]
        s = _dot_nt(q[:, lo:lo + MEM_HEAD_DIM], k) * scale
        p = jnp.exp(s - jnp.max(s, axis=-1, keepdims=True))
        p = p / jnp.sum(p, axis=-1, keepdims=True)
        o_ref[0, :, lo:lo + MEM_HEAD_DIM] = _dot(p.astype(BF16), v).astype(o_ref.dtype)


def _mem_attn(xb, wq, kvm, *, tm=512):
    bsz, seq, d = xb.shape
    n_mem = kvm.shape[1]
    tm = min(tm, seq)
    return pl.pallas_call(
        _mem_attn_kernel,
        grid=(bsz, seq // tm),
        in_specs=[pl.BlockSpec((1, tm, d), lambda b, i: (b, i, 0)),
                  pl.BlockSpec((d, d), lambda b, i: (0, 0)),
                  pl.BlockSpec((1, n_mem, 2 * d), lambda b, i: (b, 0, 0))],
        out_specs=pl.BlockSpec((1, tm, d), lambda b, i: (b, i, 0)),
        out_shape=jax.ShapeDtypeStruct((bsz, seq, d), BF16),
        compiler_params=_params("parallel", "arbitrary"),
        name="mem_attn",
    )(xb, wq, kvm)


def _s5_tables(a_re, a_im, log_dt, b_re, b_im, c_re, c_im, n_chunks):
    hi = lax.Precision.HIGHEST
    G, P, GS, T = S5_GROUPS, S5_STATE, S5_GROUP, S5_CHUNK
    dt = jnp.exp(log_dt.astype(F32))[:, None]
    lr, li = a_re.astype(F32), a_im.astype(F32)
    mag = jnp.exp(lr * dt)
    ab_re, ab_im = mag * jnp.cos(li * dt), mag * jnp.sin(li * dt)
    den = lr * lr + li * li
    nr, ni = ab_re - 1.0, ab_im
    f_re = (nr * lr + ni * li) / den
    f_im = (ni * lr - nr * li) / den
    br, bi = b_re.astype(F32), b_im.astype(F32)
    bb_re = f_re[..., None] * br - f_im[..., None] * bi
    bb_im = f_re[..., None] * bi + f_im[..., None] * br

    def apow(n):
        n = jnp.asarray(n, F32)[:, None, None]
        m = jnp.exp(lr * dt * n)
        return m * jnp.cos(li * dt * n), m * jnp.sin(li * dt * n)

    pw_re, pw_im = apow(np.arange(T + 1))
    abb_re = pw_re[..., None] * bb_re - pw_im[..., None] * bb_im
    abb_im = pw_re[..., None] * bb_im + pw_im[..., None] * bb_re

    rev = np.arange(T - 1, -1, -1)
    w_inc = jnp.concatenate([abb_re[rev], abb_im[rev]], axis=2)
    w_inc = w_inc.transpose(1, 0, 3, 2).reshape(G, T * GS, 2 * P)

    cr, ci = c_re.astype(F32), c_im.astype(F32)
    kern = (jnp.einsum("ghp,ngpk->nghk", cr, abb_re[:T], precision=hi)
            - jnp.einsum("ghp,ngpk->nghk", ci, abb_im[:T], precision=hi))
    tau = np.arange(T)[None, :] - np.arange(T)[:, None]
    toep = jnp.where((tau >= 0)[:, :, None, None, None], kern[np.clip(tau, 0, T - 1)], 0.0)
    w_toep = toep.transpose(2, 0, 4, 1, 3).reshape(G, T * GS, T * GS)

    pr, pi = pw_re[1:], pw_im[1:]
    car_re = cr[None] * pr[:, :, None, :] - ci[None] * pi[:, :, None, :]
    car_im = cr[None] * pi[:, :, None, :] + ci[None] * pr[:, :, None, :]
    w_car = jnp.concatenate([car_re, -car_im], axis=3)
    w_car = w_car.transpose(1, 3, 0, 2).reshape(G, 2 * P, T * GS)

    n_steps = max(1, int(math.ceil(math.log2(n_chunks))))
    sr, si = apow(T * (2 ** np.arange(n_steps)))
    apow_a = jnp.concatenate([sr, sr], axis=2).transpose(1, 0, 2)
    apow_b = jnp.concatenate([-si, si], axis=2).transpose(1, 0, 2)
    return w_inc.astype(BF16), w_toep.astype(BF16), w_car.astype(BF16), apow_a, apow_b


def _s5_kernel(u_ref, winc_ref, wtoep_ref, wcar_ref, pa_ref, pb_ref, d_ref, o_ref, *, n_steps):
    T, GS, GB = S5_CHUNK, S5_GROUP, S5_GROUP_BLOCK
    n_chunks = u_ref.shape[1] // T
    rows = lax.broadcasted_iota(jnp.int32, (n_chunks, 2 * S5_STATE), 0)
    xs = [u_ref[0, pl.ds(j, n_chunks, stride=T), :] for j in range(T)]
    ys = []
    for g in range(GB):
        ub = jnp.concatenate([x[:, g * GS:(g + 1) * GS] for x in xs], axis=1).astype(BF16)
        s = _dot(ub, winc_ref[g])
        for k in range(n_steps):
            d = 1 << k
            sh = jnp.where(rows >= d, pltpu.roll(s, d, 0), 0.0)
            sw = pltpu.roll(sh, S5_STATE, 1)
            s = s + sh * pa_ref[g, k:k + 1, :] + sw * pb_ref[g, k:k + 1, :]
        s_prev = jnp.where(rows >= 1, pltpu.roll(s, 1, 0), 0.0)
        ys.append(_dot(ub, wtoep_ref[g]) + _dot(s_prev.astype(BF16), wcar_ref[g]))
    for t in range(T):
        y = jnp.concatenate([yg[:, t * GS:(t + 1) * GS] for yg in ys], axis=1)
        y = y + d_ref[...] * xs[t]
        o_ref[0, pl.ds(t, n_chunks, stride=T), :] = jax.nn.gelu(y)


def _s5_core(u, tables, d_skip):
    bsz, seq, d = u.shape
    w_inc, w_toep, w_car, apow_a, apow_b = tables
    n_steps = apow_a.shape[1]
    gb = S5_GROUP_BLOCK
    blk = lambda b, i: (b, 0, i)
    wblk = lambda b, i: (i, 0, 0)
    return pl.pallas_call(
        functools.partial(_s5_kernel, n_steps=n_steps),
        grid=(bsz, S5_GROUPS // gb),
        in_specs=[pl.BlockSpec((1, seq, LANES), blk),
                  pl.BlockSpec((gb,) + w_inc.shape[1:], wblk),
                  pl.BlockSpec((gb,) + w_toep.shape[1:], wblk),
                  pl.BlockSpec((gb,) + w_car.shape[1:], wblk),
                  pl.BlockSpec((gb,) + apow_a.shape[1:], wblk),
                  pl.BlockSpec((gb,) + apow_b.shape[1:], wblk),
                  pl.BlockSpec((1, LANES), lambda b, i: (0, i))],
        out_specs=pl.BlockSpec((1, seq, LANES), blk),
        out_shape=jax.ShapeDtypeStruct(u.shape, F32),
        compiler_params=_params("parallel", "arbitrary"),
        name="s5_scan",
    )(u, w_inc, w_toep, w_car, apow_a, apow_b, d_skip.astype(F32).reshape(1, d))


def _kv_proj_kernel(x_ref, w_ref, o_ref):
    acc = _dot(x_ref[0], w_ref[...])
    for s in range(o_ref.shape[1]):
        o_ref[0, s] = acc[:, s * NSA_HEAD_DIM:(s + 1) * NSA_HEAD_DIM].astype(o_ref.dtype)


def _kv_proj(xb, w, out_dtype, *, tm=1024, slabs=8):
    bsz, seq, d = xb.shape
    n_slabs = w.shape[1] // NSA_HEAD_DIM
    tm = min(tm, seq)
    return pl.pallas_call(
        _kv_proj_kernel,
        grid=(bsz, seq // tm, n_slabs // slabs),
        in_specs=[pl.BlockSpec((1, tm, d), lambda b, i, j: (b, i, 0)),
                  pl.BlockSpec((d, slabs * NSA_HEAD_DIM), lambda b, i, j: (0, j))],
        out_specs=pl.BlockSpec((1, slabs, tm, NSA_HEAD_DIM), lambda b, i, j: (b, j, i, 0)),
        out_shape=jax.ShapeDtypeStruct((bsz, n_slabs, seq, NSA_HEAD_DIM), out_dtype),
        compiler_params=_params("parallel", "parallel", "arbitrary"),
        name="kv_proj",
    )(xb, w)


def _compress_kernel(t_ref, pos_ref, w1_ref, w2_ref, o_ref):
    t = t_ref[0, 0]
    n = t.shape[0]
    half = CMP_STRIDE * NSA_HEAD_DIM
    first = _dot((t + pos_ref[0, 0:1, :]).astype(BF16), w1_ref[0, :half, :])
    second = _dot((t + pos_ref[0, 1:2, :]).astype(BF16), w1_ref[0, half:, :])
    pre = first + pltpu.roll(second, n - 1, 0)
    out = _dot(jax.nn.gelu(pre).astype(BF16), w2_ref[0])
    rows = lax.broadcasted_iota(jnp.int32, out.shape, 0)
    o_ref[0, 0] = jnp.where(rows < n - 1, out, 0.0).astype(o_ref.dtype)


def _compress(kv_cmp, pos, w1, w2):
    bsz, n_slots, seq, hd = kv_cmp.shape
    n16 = seq // CMP_STRIDE
    t16 = kv_cmp.reshape(bsz, n_slots, n16, CMP_STRIDE * hd)
    kvh = NSA_KV_HEADS
    return pl.pallas_call(
        _compress_kernel,
        grid=(bsz, n_slots),
        in_specs=[pl.BlockSpec((1, 1, n16, CMP_STRIDE * hd), lambda b, s: (b, s, 0, 0)),
                  pl.BlockSpec((1, 2, CMP_STRIDE * hd), lambda b, s: (s // kvh, 0, 0)),
                  pl.BlockSpec((1, CMP_LEN * hd, hd), lambda b, s: (s // kvh, 0, 0)),
                  pl.BlockSpec((1, hd, hd), lambda b, s: (s // kvh, 0, 0))],
        out_specs=pl.BlockSpec((1, 1, n16, hd), lambda b, s: (b, s, 0, 0)),
        out_shape=jax.ShapeDtypeStruct((bsz, n_slots, n16, hd), BF16),
        compiler_params=_params("parallel", "arbitrary"),
        name="nsa_compress",
    )(t16, pos, w1, w2)


def _nsa_kernel(q_ref, g_ref, kc_ref, vc_ref, ks_ref, vst_ref, kw_ref, vw_ref, o_ref,
                kaug_ref, band_ref, selmap_ref):
    qb = pl.program_id(2)
    s0 = qb * Q_BLOCK
    R, hd, QB = NSA_REP, NSA_HEAD_DIM, Q_BLOCK
    rows = R * QB
    n_cmp = kc_ref.shape[2]
    seq = ks_ref.shape[2]
    n_sel = seq // SEL_LEN
    n_win = band_ref.shape[1]

    @pl.when(qb == 0)
    def _():
        lane = lax.broadcasted_iota(jnp.int32, (SEL_TILE, n_sel), 1)
        blk = lax.broadcasted_iota(jnp.int32, (SEL_TILE, n_sel), 0) // SEL_LEN

        def fill(j, _):
            base = pl.multiple_of(j * SEL_TILE, SEL_TILE)
            onehot = jnp.where(lane == blk + j * (SEL_TILE // SEL_LEN), 1.0, 0.0).astype(BF16)
            kaug_ref[pl.ds(base, SEL_TILE), :] = jnp.concatenate(
                [ks_ref[0, 0, pl.ds(base, SEL_TILE), :], onehot], axis=1)
            return 0

        lax.fori_loop(0, seq // SEL_TILE, fill, 0)
        qi = lax.broadcasted_iota(jnp.int32, band_ref.shape, 0)
        col = lax.broadcasted_iota(jnp.int32, band_ref.shape, 1)
        band_ref[...] = jnp.where((col > qi) & (col <= qi + WINDOW), 0.0, NEG_BIG)
        sel_s = lax.broadcasted_iota(jnp.int32, (n_sel, n_cmp), 0) * SEL_LEN
        cmp_c = lax.broadcasted_iota(jnp.int32, (n_sel, n_cmp), 1) * CMP_STRIDE
        selmap_ref[...] = jnp.where((cmp_c < sel_s + SEL_LEN) & (cmp_c + CMP_LEN > sel_s),
                                    1.0, 0.0).astype(BF16)

    q4 = jnp.concatenate([q_ref[0, :, r * hd:(r + 1) * hd] for r in range(R)], axis=0)
    t_rows = s0 + lax.broadcasted_iota(jnp.int32, (rows, 1), 0) % QB

    cmp_end =lax.broadcasted_iota(jnp.int32, (1, n_cmp), 1) * CMP_STRIDE + (CMP_LEN - 1)
    sc = jnp.where(cmp_end <= t_rows, _dot_nt(q4, kc_ref[0, 0]), NEG_BIG)
    mc = jnp.max(sc, axis=-1, keepdims=True)
    pc = jnp.exp(sc - mc)
    lc = jnp.sum(pc, axis=-1, keepdims=True)
    pc = pc * jnp.where(mc > 0.5 * NEG_BIG, 1.0 / jnp.maximum(lc, 1e-30), 0.0)
    o_cmp = _dot(pc.astype(BF16), vc_ref[0, 0])

    pc_sum = pc[0:QB]
    for r in range(1, R):
        pc_sum = pc_sum + pc[r * QB:(r + 1) * QB]
    pc_hi = pc_sum.astype(BF16)
    pc_lo = (pc_sum - pc_hi.astype(F32)).astype(BF16)
    sel_map_t = selmap_ref[...]
    imp =_dot_nt(sel_map_t, pc_hi) + _dot_nt(sel_map_t, pc_lo)

    sid = lax.broadcasted_iota(jnp.int32, (n_sel, QB), 0)
    sid_f = sid.astype(F32)
    tq = s0 + lax.broadcasted_iota(jnp.int32, (n_sel, QB), 1)
    cur = tq // SEL_LEN
    forced = (sid == 0) | (sid == cur) | (sid == cur - 1)
    imp = jnp.where(forced, SEL_FORCE, jnp.where(sid * SEL_LEN <= tq, imp, -SEL_FORCE))

    top_n = min(SEL_TOPN, n_sel)
    sel = jnp.zeros((n_sel, QB), F32)
    for _ in range(top_n):
        best = jnp.max(imp, axis=0, keepdims=True)
        first = jnp.min(jnp.where(imp == best, sid_f, float(n_sel)), axis=0, keepdims=True)
        hit = sid_f == first
        sel = jnp.where(hit, 1.0, sel)
        imp = jnp.where(hit, -jnp.inf, imp)
    bias_t = jnp.where(sel > 0.5, 0.0, NEG_BIG).astype(BF16)
    q_t = [q_ref[0, :, r * hd:(r + 1) * hd].astype(F32).T.astype(BF16) for r in range(R)]
    q_aug_t = jnp.concatenate([jnp.concatenate(q_t, axis=1),
                               jnp.concatenate([bias_t] * R, axis=1)], axis=0)
    t_lanes = s0 + lax.broadcasted_iota(jnp.int32, (1, rows), 1) % QB

    def sel_tile(j, carry, causal):
        m_prev, l_prev, acc = carry
        base = pl.multiple_of(j * SEL_TILE, SEL_TILE)
        s = _dot(kaug_ref[pl.ds(base, SEL_TILE), :], q_aug_t)
        if causal:
            kpos = base + lax.broadcasted_iota(jnp.int32, (SEL_TILE, 1), 0)
            s = jnp.where(kpos <= t_lanes, s, NEG_BIG)
        m_new = jnp.maximum(m_prev, jnp.max(s, axis=0, keepdims=True))
        corr = jnp.exp(m_prev - m_new)
        p = jnp.exp(s - m_new)
        l_new = l_prev * corr + jnp.sum(p, axis=0, keepdims=True)
        acc = acc * corr + _dot(vst_ref[0, 0, :, pl.ds(base, SEL_TILE)], p.astype(BF16))
        return m_new, l_new, acc

    n_past = s0 // SEL_TILE
    init = (jnp.full((1, rows), NEG_BIG, F32), jnp.zeros((1, rows), F32), jnp.zeros((hd, rows), F32))
    carry = lax.fori_loop(0, n_past, functools.partial(sel_tile, causal=False), init)
    _, l_sel, acc_sel = sel_tile(n_past, carry, True)
    o_slc_t = acc_sel * (1.0 / l_sel)
    o_slc = jnp.concatenate([o_slc_t[:, r * QB:(r + 1) * QB].T for r in range(R)], axis=0)

    WB = band_ref.shape[0]
    col = lax.broadcasted_iota(jnp.int32, (1, n_win), 1)
    o_win_sub = []
    for sb in range(QB // WB):
        q_sb = jnp.concatenate([q4[r * QB + sb * WB:r * QB + (sb + 1) * WB] for r in range(R)], axis=0)
        start = pl.multiple_of(s0 + sb * WB, WB)
        win_bias = band_ref[...] + jnp.where(col + start >= WINDOW, 0.0, NEG_BIG)
        sw = _dot_nt(q_sb, kw_ref[0, 0, pl.ds(start, n_win), :])
        sw = jnp.concatenate([sw[r * WB:(r + 1) * WB] + win_bias for r in range(R)], axis=0)
        pw = jnp.exp(sw - jnp.max(sw, axis=-1, keepdims=True))
        lw = jnp.sum(pw, axis=-1, keepdims=True)
        o_win_sub.append(_dot(pw.astype(BF16), vw_ref[0, 0, pl.ds(start, n_win), :]) * (1.0 / lw))
    o_win = jnp.concatenate([o[r * WB:(r + 1) * WB] for r in range(R) for o in o_win_sub], axis=0)

    gates = g_ref[0]
    for r in range(R):
        sl = slice(r * QB, (r + 1) * QB)
        c = r * N_GATES
        out = (gates[:, c:c + 1] * o_cmp[sl] + gates[:, c + 1:c + 2] * o_slc[sl]
               + gates[:, c + 2:c + 3] * o_win[sl])
        o_ref[0, :, r * hd:(r + 1) * hd] = out.astype(o_ref.dtype)


def _nsa_attention(q, gates, kv_cmp, kv_sel, kv_win):
    bsz, seq, _ = q.shape
    G, hd = NSA_KV_HEADS, NSA_HEAD_DIM
    n_cmp = kv_cmp.shape[2]
    gw = NSA_REP * hd
    n_win = WINDOW + WIN_BLOCK
    v_sel_t = jnp.swapaxes(kv_sel[:, G:2 * G], 2, 3)

    def slot(k):
        return lambda b, g, i: (b, k * G + g, 0, 0)

    return pl.pallas_call(
        _nsa_kernel,
        grid=(bsz, G, seq // Q_BLOCK),
        in_specs=[pl.BlockSpec((1, Q_BLOCK, gw), lambda b, g, i: (b, i, g)),
                  pl.BlockSpec((1, Q_BLOCK, LANES), lambda b, g, i: (b, i, g)),
                  pl.BlockSpec((1, 1, n_cmp, hd), slot(0)),
                  pl.BlockSpec((1, 1, n_cmp, hd), slot(1)),
                  pl.BlockSpec((1, 1, seq, hd), slot(0)),
                  pl.BlockSpec((1, 1, hd, seq), lambda b, g, i: (b, g, 0, 0)),
                  pl.BlockSpec((1, 1, WINDOW + seq, hd), slot(0)),
                  pl.BlockSpec((1, 1, WINDOW + seq, hd), slot(1))],
        out_specs=pl.BlockSpec((1, Q_BLOCK, gw), lambda b, g, i: (b, i, g)),
        out_shape=jax.ShapeDtypeStruct(q.shape, BF16),
        scratch_shapes=[pltpu.VMEM((seq, hd + seq // SEL_LEN), BF16),
                        pltpu.VMEM((WIN_BLOCK, n_win), F32),
                        pltpu.VMEM((seq // SEL_LEN, n_cmp), BF16)],
        compiler_params=_params("arbitrary", "arbitrary", "arbitrary"),
        name="nsa_attention",
    )(q, gates, kv_cmp, kv_cmp, kv_sel, v_sel_t, kv_win, kv_win)


def _s5_layer(x, xin, bsz, seq, w_in, tables, d_skip, w_glu, w_out, g, b):
    m = bsz * seq
    u = _mm(xin, w_in, F32, tm=1024, tn=1024)
    y = _s5_core(u.reshape(bsz, seq, D_MODEL), tables, d_skip)
    z = _mm_glu(y.reshape(m, D_MODEL), w_glu)
    return _mm_res_ln(z, w_out, x, g, b)


def _nsa_layer(x, xb, bsz, seq, kv_cmp, kv_sel, kv_win, w_q, w_gate, w_o, g, b):
    m = bsz * seq
    q = _mm(xb, w_q, BF16, tm=1024, tn=1024, scale=NSA_HEAD_DIM ** -0.5)
    gates = _mm(xb, w_gate, F32, tm=1024, tn=512, act="sigmoid")
    o = _nsa_attention(q.reshape(bsz, seq, -1), gates.reshape(bsz, seq, -1), kv_cmp, kv_sel, kv_win)
    return _mm_res_ln(o.reshape(m, -1), w_o, x, g, b)


def _gate_weights(w_qg):
    n_q = NSA_HEADS * NSA_HEAD_DIM
    per = NSA_REP * N_GATES
    wg = w_qg[:, n_q:].reshape(-1, NSA_KV_HEADS, per)
    wg = jnp.pad(wg, ((0, 0), (0, 0), (0, LANES - per)))
    return wg.reshape(-1, NSA_KV_HEADS * LANES).astype(BF16)


def kernel(x, mem, s5_w_in, s5_a_re, s5_a_im, s5_log_dt, s5_b_re, s5_b_im, s5_c_re, s5_c_im, s5_d, s5_w_glu, s5_w_out, kv_w, cmp_pos_k, cmp_w1_k, cmp_w2_k, cmp_pos_v, cmp_w1_v, cmp_w2_v, nsa_w_qg, nsa_w_o, mem_w_q, mem_w_kv, mem_w_o, mlp_w_up, mlp_w_down, ln_g, ln_b):
    bsz, seq, d = x.shape
    m = bsz * seq
    bf = lambda w: w.astype(BF16)
    xf = x.reshape(m, d)
    xin = xf
    memf = mem.reshape(-1, d)
    kv_cmp = kv_sel = kv_win = None
    for layer in range(DEPTH):
        if layer < N_A_LAYERS:
            i = layer
            tables = _s5_tables(s5_a_re[i], s5_a_im[i], s5_log_dt[i], s5_b_re[i], s5_b_im[i],
                                s5_c_re[i], s5_c_im[i], seq // S5_CHUNK)
            xf, xin = _s5_layer(xf, xin, bsz, seq, bf(s5_w_in[i]), tables, s5_d[i], bf(s5_w_glu[i]),
                                bf(s5_w_out[i]), ln_g[layer, 0], ln_b[layer, 0])
        else:
            i = layer - N_A_LAYERS
            if xin.dtype != BF16:
                xin = xin.astype(BF16)
            if i == 0:
                n_cmp_cols = 2 * NSA_KV_HEADS * NSA_HEAD_DIM
                x3 = xin.reshape(bsz, seq, d)
                kv_c = _kv_proj(x3, bf(kv_w[:, :n_cmp_cols]), F32)
                kv_sel = _kv_proj(x3, bf(kv_w[:, n_cmp_cols:]), BF16)
                kv_win = jnp.pad(kv_sel[:, 2 * NSA_KV_HEADS:], ((0, 0), (0, 0), (WINDOW, 0), (0, 0)))
                half = CMP_STRIDE * NSA_HEAD_DIM
                pos = jnp.stack([cmp_pos_k.reshape(2, half), cmp_pos_v.reshape(2, half)])
                kv_cmp = _compress(kv_c, pos, jnp.stack([bf(cmp_w1_k), bf(cmp_w1_v)]),
                                   jnp.stack([bf(cmp_w2_k), bf(cmp_w2_v)]))
            n_q = NSA_HEADS * NSA_HEAD_DIM
            xf, xin = _nsa_layer(xf, xin, bsz, seq, kv_cmp, kv_sel, kv_win, bf(nsa_w_qg[i][:, :n_q]),
                                 _gate_weights(nsa_w_qg[i]), bf(nsa_w_o[i]),
                                 ln_g[layer, 0], ln_b[layer, 0])
        kvm = _mm(memf, bf(mem_w_kv[layer]), BF16, tm=512, tn=1024)
        o = _mem_attn(xin.reshape(bsz, seq, d), bf(mem_w_q[layer]), kvm.reshape(bsz, -1, 2 * d))
        xf, xin = _mm_res_ln(o.reshape(m, d), bf(mem_w_o[layer]), xf, ln_g[layer, 1], ln_b[layer, 1])
        xf, xin = _mlp_ln(xin, xf, bf(mlp_w_up[layer]), bf(mlp_w_down[layer]),
                          ln_g[layer, 2], ln_b[layer, 2])
    return xf.reshape(bsz, seq, d)
```

```python
import functools
import math

import jax
import jax.numpy as jnp
import numpy as np
from jax import lax
from jax.experimental import pallas as pl
from jax.experimental.pallas import tpu as pltpu

F32 = jnp.float32
BF16 = jnp.bfloat16

D_MODEL = 2048
DEPTH = 2
N_A_LAYERS = DEPTH // 2

S5_GROUP = 16
S5_GROUPS = D_MODEL // S5_GROUP
S5_STATE = 64
S5_CHUNK = 16
S5_ROW = S5_CHUNK * S5_GROUP
S5_GROUP_BLOCK = 8

NSA_HEADS = 16
NSA_HEAD_DIM = D_MODEL // NSA_HEADS
NSA_KV_HEADS = 4
NSA_REP = NSA_HEADS // NSA_KV_HEADS
CMP_LEN = 32
CMP_STRIDE = 16
SEL_LEN = 64
SEL_TOPN = 16
WINDOW = 512
Q_BLOCK = 512
N_GATES = 3
N_KV_SLOTS = 6
SEL_TILE = 1024
WIN_BLOCK = 128

MEM_HEADS = 4
MEM_HEAD_DIM = D_MODEL // MEM_HEADS

D_FF = 4 * D_MODEL

DN_ALPHA = float((2 * DEPTH) ** 0.25)
LN_EPS = 1e-5
NEG_BIG = -1e30
SEL_FORCE = 1e9

LANES = 128
VMEM_LIMIT = 56 * 1024 * 1024


def _params(*semantics):
    return pltpu.CompilerParams(dimension_semantics=semantics, vmem_limit_bytes=VMEM_LIMIT)


def _layer_norm(y, g, b):
    mu = jnp.mean(y, axis=-1, keepdims=True)
    yc = y - mu
    var = jnp.mean(yc * yc, axis=-1, keepdims=True)
    return yc * lax.rsqrt(var + LN_EPS) * g + b


def _dot(a, b):
    return jnp.dot(a, b, preferred_element_type=F32)


def _dot_nt(a, b):
    return lax.dot_general(a, b, (((1,), (1,)), ((), ())), preferred_element_type=F32)


def _mm_kernel(x_ref, w_ref, o_ref, *, scale, act):
    acc = _dot(x_ref[...].astype(BF16), w_ref[...])
    if scale is not None:
        acc = acc * scale
    if act == "sigmoid":
        acc = jax.nn.sigmoid(acc)
    o_ref[...] = acc.astype(o_ref.dtype)


def _mm(x, w, out_dtype, *, tm, tn, scale=None, act=None):
    m, k = x.shape
    n = w.shape[1]
    tm, tn = min(tm, m), min(tn, n)
    return pl.pallas_call(
        functools.partial(_mm_kernel, scale=scale, act=act),
        grid=(m // tm, n // tn),
        in_specs=[pl.BlockSpec((tm, k), lambda i, j: (i, 0)),
                  pl.BlockSpec((k, tn), lambda i, j: (0, j))],
        out_specs=pl.BlockSpec((tm, tn), lambda i, j: (i, j)),
        out_shape=jax.ShapeDtypeStruct((m, n), out_dtype),
        compiler_params=_params("parallel", "arbitrary"),
        name="mm",
    )(x, w)


def _glu_kernel(y_ref, wv_ref, wg_ref, o_ref):
    y = y_ref[...].astype(BF16)
    val = _dot(y, wv_ref[...])
    gate = _dot(y, wg_ref[...])
    o_ref[...] = (val * jax.nn.sigmoid(gate)).astype(o_ref.dtype)


def _mm_glu(y, w_glu, *, tm=1024, tn=512):
    m, k = y.shape
    n = w_glu.shape[1] // 2
    tm = min(tm, m)
    nb = n // tn
    return pl.pallas_call(
        _glu_kernel,
        grid=(m // tm, nb),
        in_specs=[pl.BlockSpec((tm, k), lambda i, j: (i, 0)),
                  pl.BlockSpec((k, tn), lambda i, j: (0, j)),
                  pl.BlockSpec((k, tn), lambda i, j: (0, j + nb))],
        out_specs=pl.BlockSpec((tm, tn), lambda i, j: (i, j)),
        out_shape=jax.ShapeDtypeStruct((m, n), BF16),
        compiler_params=_params("parallel", "arbitrary"),
        name="mm_glu",
    )(y, w_glu, w_glu)


def _res_ln_kernel(z_ref, w_ref, x_ref, g_ref, b_ref, o_ref, ob_ref):
    h = _dot(z_ref[...], w_ref[...])
    y = _layer_norm(DN_ALPHA * x_ref[...] + h, g_ref[...], b_ref[...])
    o_ref[...] = y
    ob_ref[...] = y.astype(BF16)


def _mm_res_ln(z, w, x, g, b, *, tm=512):
    m, k = z.shape
    d = w.shape[1]
    tm = min(tm, m)
    row = lambda i: (i, 0)
    fixed = lambda i: (0, 0)
    return pl.pallas_call(
        _res_ln_kernel,
        grid=(m // tm,),
        in_specs=[pl.BlockSpec((tm, k), row), pl.BlockSpec((k, d), fixed),
                  pl.BlockSpec((tm, d), row), pl.BlockSpec((1, d), fixed),
                  pl.BlockSpec((1, d), fixed)],
        out_specs=[pl.BlockSpec((tm, d), row), pl.BlockSpec((tm, d), row)],
        out_shape=[jax.ShapeDtypeStruct((m, d), F32), jax.ShapeDtypeStruct((m, d), BF16)],
        compiler_params=_params("parallel"),
        name="proj_postnorm",
    )(z, w, x, g.reshape(1, d), b.reshape(1, d))


def _mlp_kernel(xb_ref, x_ref, wu_ref, wd_ref, g_ref, b_ref, o_ref, ob_ref):
    j = pl.program_id(1)
    h = jnp.maximum(_dot(xb_ref[...], wu_ref[...]), 0.0)
    part = _dot((h * h).astype(BF16), wd_ref[...])

    @pl.when(j == 0)
    def _():
        o_ref[...] = DN_ALPHA * x_ref[...] + part

    @pl.when(j > 0)
    def _():
        o_ref[...] += part

    @pl.when(j == pl.num_programs(1) - 1)
    def _():
        y = _layer_norm(o_ref[...], g_ref[...], b_ref[...])
        o_ref[...] = y
        ob_ref[...] = y.astype(BF16)


def _mlp_ln(xb, x, w_up, w_down, g, b, *, tm=512, tf=1024):
    m, d = x.shape
    ff = w_up.shape[1]
    tm = min(tm, m)
    row = lambda i, j: (i, 0)
    fixed = lambda i, j: (0, 0)
    return pl.pallas_call(
        _mlp_kernel,
        grid=(m // tm, ff // tf),
        in_specs=[pl.BlockSpec((tm, d), row), pl.BlockSpec((tm, d), row),
                  pl.BlockSpec((d, tf), lambda i, j: (0, j)),
                  pl.BlockSpec((tf, d), lambda i, j: (j, 0)),
                  pl.BlockSpec((1, d), fixed), pl.BlockSpec((1, d), fixed)],
        out_specs=[pl.BlockSpec((tm, d), row), pl.BlockSpec((tm, d), row)],
        out_shape=[jax.ShapeDtypeStruct((m, d), F32), jax.ShapeDtypeStruct((m, d), BF16)],
        compiler_params=_params("parallel", "arbitrary"),
        name="mlp_postnorm",
    )(xb, x, w_up, w_down, g.reshape(1, d), b.reshape(1, d))


def _mem_attn_kernel(x_ref, wq_ref, kv_ref, o_ref):
    q = _dot(x_ref[0], wq_ref[...]).astype(BF16)
    scale = MEM_HEAD_DIM ** -0.5
    for h in range(MEM_HEADS):
        lo = h * MEM_HEAD_DIM
        k = kv_ref[0, :, lo:lo + MEM_HEAD_DIM]
        v = kv_ref[0, :, D_MODEL + lo:D_MODEL + lo + MEM_HEAD_DIM]
        s = _dot_nt(q[:, lo:lo + MEM_HEAD_DIM], k) * scale
        p = jnp.exp(s - jnp.max(s, axis=-1, keepdims=True))
        p = p / jnp.sum(p, axis=-1, keepdims=True)
        o_ref[0, :, lo:lo + MEM_HEAD_DIM] = _dot(p.astype(BF16), v).astype(o_ref.dtype)


def _mem_attn(xb, wq, kvm, *, tm=512):
    bsz, seq, d = xb.shape
    n_mem = kvm.shape[1]
    tm = min(tm, seq)
    return pl.pallas_call(
        _mem_attn_kernel,
        grid=(bsz, seq // tm),
        in_specs=[pl.BlockSpec((1, tm, d), lambda b, i: (b, i, 0)),
                  pl.BlockSpec((d, d), lambda b, i: (0, 0)),
                  pl.BlockSpec((1, n_mem, 2 * d), lambda b, i: (b, 0, 0))],
        out_specs=pl.BlockSpec((1, tm, d), lambda b, i: (b, i, 0)),
        out_shape=jax.ShapeDtypeStruct((bsz, seq, d), BF16),
        compiler_params=_params("parallel", "arbitrary"),
        name="mem_attn",
    )(xb, wq, kvm)


def _s5_tables(a_re, a_im, log_dt, b_re, b_im, c_re, c_im, n_chunks):
    hi = lax.Precision.HIGHEST
    G, P, GS, T = S5_GROUPS, S5_STATE, S5_GROUP, S5_CHUNK
    dt = jnp.exp(log_dt.astype(F32))[:, None]
    lr, li = a_re.astype(F32), a_im.astype(F32)
    mag = jnp.exp(lr * dt)
    ab_re, ab_im = mag * jnp.cos(li * dt), mag * jnp.sin(li * dt)
    den = lr * lr + li * li
    nr, ni = ab_re - 1.0, ab_im
    f_re = (nr * lr + ni * li) / den
    f_im = (ni * lr - nr * li) / den
    br, bi = b_re.astype(F32), b_im.astype(F32)
    bb_re = f_re[..., None] * br - f_im[..., None] * bi
    bb_im = f_re[..., None] * bi + f_im[..., None] * br

    def apow(n):
        n = jnp.asarray(n, F32)[:, None, None]
        m = jnp.exp(lr * dt * n)
        return m * jnp.cos(li * dt * n), m * jnp.sin(li * dt * n)

    pw_re, pw_im = apow(np.arange(T + 1))
    abb_re = pw_re[..., None] * bb_re - pw_im[..., None] * bb_im
    abb_im = pw_re[..., None] * bb_im + pw_im[..., None] * bb_re

    rev = np.arange(T - 1, -1, -1)
    w_inc = jnp.concatenate([abb_re[rev], abb_im[rev]], axis=2)
    w_inc = w_inc.transpose(1, 0, 3, 2).reshape(G, T * GS, 2 * P)

    cr, ci = c_re.astype(F32), c_im.astype(F32)
    kern = (jnp.einsum("ghp,ngpk->nghk", cr, abb_re[:T], precision=hi)
            - jnp.einsum("ghp,ngpk->nghk", ci, abb_im[:T], precision=hi))
    tau = np.arange(T)[None, :] - np.arange(T)[:, None]
    toep = jnp.where((tau >= 0)[:, :, None, None, None], kern[np.clip(tau, 0, T - 1)], 0.0)
    w_toep = toep.transpose(2, 0, 4, 1, 3).reshape(G, T * GS, T * GS)

    pr, pi = pw_re[1:], pw_im[1:]
    car_re = cr[None] * pr[:, :, None, :] - ci[None] * pi[:, :, None, :]
    car_im = cr[None] * pi[:, :, None, :] + ci[None] * pr[:, :, None, :]
    w_car = jnp.concatenate([car_re, -car_im], axis=3)
    w_car = w_car.transpose(1, 3, 0, 2).reshape(G, 2 * P, T * GS)

    n_steps = max(1, int(math.ceil(math.log2(n_chunks))))
    sr, si = apow(T * (2 ** np.arange(n_steps)))
    apow_a = jnp.concatenate([sr, sr], axis=2).transpose(1, 0, 2)
    apow_b = jnp.concatenate([-si, si], axis=2).transpose(1, 0, 2)
    return w_inc.astype(BF16), w_toep.astype(BF16), w_car.astype(BF16), apow_a, apow_b


def _s5_kernel(u_ref, winc_ref, wtoep_ref, wcar_ref, pa_ref, pb_ref, d_ref, o_ref, *, n_steps):
    T, GS, GB = S5_CHUNK, S5_GROUP, S5_GROUP_BLOCK
    n_chunks = u_ref.shape[1] // T
    rows = lax.broadcasted_iota(jnp.int32, (n_chunks, 2 * S5_STATE), 0)
    xs = [u_ref[0, pl.ds(j, n_chunks, stride=T), :] for j in range(T)]
    ys = []
    for g in range(GB):
        ub = jnp.concatenate([x[:, g * GS:(g + 1) * GS] for x in xs], axis=1).astype(BF16)
        s = _dot(ub, winc_ref[g])
        for k in range(n_steps):
            d = 1 << k
            sh = jnp.where(rows >= d, pltpu.roll(s, d, 0), 0.0)
            sw = pltpu.roll(sh, S5_STATE, 1)
            s = s + sh * pa_ref[g, k:k + 1, :] + sw * pb_ref[g, k:k + 1, :]
        s_prev = jnp.where(rows >= 1, pltpu.roll(s, 1, 0), 0.0)
        ys.append(_dot(ub, wtoep_ref[g]) + _dot(s_prev.astype(BF16), wcar_ref[g]))
    for t in range(T):
        y = jnp.concatenate([yg[:, t * GS:(t + 1) * GS] for yg in ys], axis=1)
        y = y + d_ref[...] * xs[t]
        o_ref[0, pl.ds(t, n_chunks, stride=T), :] = jax.nn.gelu(y)


def _s5_core(u, tables, d_skip):
    bsz, seq, d = u.shape
    w_inc, w_toep, w_car, apow_a, apow_b = tables
    n_steps = apow_a.shape[1]
    gb = S5_GROUP_BLOCK
    blk = lambda b, i: (b, 0, i)
    wblk = lambda b, i: (i, 0, 0)
    return pl.pallas_call(
        functools.partial(_s5_kernel, n_steps=n_steps),
        grid=(bsz, S5_GROUPS // gb),
        in_specs=[pl.BlockSpec((1, seq, LANES), blk),
                  pl.BlockSpec((gb,) + w_inc.shape[1:], wblk),
                  pl.BlockSpec((gb,) + w_toep.shape[1:], wblk),
                  pl.BlockSpec((gb,) + w_car.shape[1:], wblk),
                  pl.BlockSpec((gb,) + apow_a.shape[1:], wblk),
                  pl.BlockSpec((gb,) + apow_b.shape[1:], wblk),
                  pl.BlockSpec((1, LANES), lambda b, i: (0, i))],
        out_specs=pl.BlockSpec((1, seq, LANES), blk),
        out_shape=jax.ShapeDtypeStruct(u.shape, F32),
        compiler_params=_params("parallel", "arbitrary"),
        name="s5_scan",
    )(u, w_inc, w_toep, w_car, apow_a, apow_b, d_skip.astype(F32).reshape(1, d))


def _kv_proj_kernel(x_ref, w_ref, o_ref):
    acc = _dot(x_ref[0], w_ref[...])
    for s in range(o_ref.shape[1]):
        o_ref[0, s] = acc[:, s * NSA_HEAD_DIM:(s + 1) * NSA_HEAD_DIM].astype(o_ref.dtype)


def _kv_proj(xb, w, out_dtype, *, tm=1024, slabs=8):
    bsz, seq, d = xb.shape
    n_slabs = w.shape[1] // NSA_HEAD_DIM
    tm = min(tm, seq)
    return pl.pallas_call(
        _kv_proj_kernel,
        grid=(bsz, seq // tm, n_slabs // slabs),
        in_specs=[pl.BlockSpec((1, tm, d), lambda b, i, j: (b, i, 0)),
                  pl.BlockSpec((d, slabs * NSA_HEAD_DIM), lambda b, i, j: (0, j))],
        out_specs=pl.BlockSpec((1, slabs, tm, NSA_HEAD_DIM), lambda b, i, j: (b, j, i, 0)),
        out_shape=jax.ShapeDtypeStruct((bsz, n_slabs, seq, NSA_HEAD_DIM), out_dtype),
        compiler_params=_params("parallel", "parallel", "arbitrary"),
        name="kv_proj",
    )(xb, w)


def _compress_kernel(t_ref, pos_ref, w1_ref, w2_ref, o_ref):
    t = t_ref[0, 0]
    n = t.shape[0]
    half = CMP_STRIDE * NSA_HEAD_DIM
    first = _dot((t + pos_ref[0, 0:1, :]).astype(BF16), w1_ref[0, :half, :])
    second = _dot((t + pos_ref[0, 1:2, :]).astype(BF16), w1_ref[0, half:, :])
    pre = first + pltpu.roll(second, n - 1, 0)
    out = _dot(jax.nn.gelu(pre).astype(BF16), w2_ref[0])
    rows = lax.broadcasted_iota(jnp.int32, out.shape, 0)
    o_ref[0, 0] = jnp.where(rows < n - 1, out, 0.0).astype(o_ref.dtype)


def _compress(kv_cmp, pos, w1, w2):
    bsz, n_slots, seq, hd = kv_cmp.shape
    n16 = seq // CMP_STRIDE
    t16 = kv_cmp.reshape(bsz, n_slots, n16, CMP_STRIDE * hd)
    kvh = NSA_KV_HEADS
    return pl.pallas_call(
        _compress_kernel,
        grid=(bsz, n_slots),
        in_specs=[pl.BlockSpec((1, 1, n16, CMP_STRIDE * hd), lambda b, s: (b, s, 0, 0)),
                  pl.BlockSpec((1, 2, CMP_STRIDE * hd), lambda b, s: (s // kvh, 0, 0)),
                  pl.BlockSpec((1, CMP_LEN * hd, hd), lambda b, s: (s // kvh, 0, 0)),
                  pl.BlockSpec((1, hd, hd), lambda b, s: (s // kvh, 0, 0))],
        out_specs=pl.BlockSpec((1, 1, n16, hd), lambda b, s: (b, s, 0, 0)),
        out_shape=jax.ShapeDtypeStruct((bsz, n_slots, n16, hd), BF16),
        compiler_params=_params("parallel", "arbitrary"),
        name="nsa_compress",
    )(t16, pos, w1, w2)


def _nsa_kernel(q_ref, g_ref, kc_ref, vc_ref, ks_ref, vst_ref, kw_ref, vw_ref, o_ref,
                kaug_ref, band_ref, selmap_ref):
    qb = pl.program_id(2)
    s0 = qb * Q_BLOCK
    R, hd, QB = NSA_REP, NSA_HEAD_DIM, Q_BLOCK
    rows = R * QB
    n_cmp = kc_ref.shape[2]
    seq = ks_ref.shape[2]
    n_sel = seq // SEL_LEN
    n_win = band_ref.shape[1]

    @pl.when(qb == 0)
    def _():
        lane = lax.broadcasted_iota(jnp.int32, (SEL_TILE, n_sel), 1)
        blk = lax.broadcasted_iota(jnp.int32, (SEL_TILE, n_sel), 0) // SEL_LEN

        def fill(j, _):
            base = pl.multiple_of(j * SEL_TILE, SEL_TILE)
            onehot = jnp.where(lane == blk + j * (SEL_TILE // SEL_LEN), 1.0, 0.0).astype(BF16)
            kaug_ref[pl.ds(base, SEL_TILE), :] = jnp.concatenate(
                [ks_ref[0, 0, pl.ds(base, SEL_TILE), :], onehot], axis=1)
            return 0

        lax.fori_loop(0, seq // SEL_TILE, fill, 0)
        qi = lax.broadcasted_iota(jnp.int32, band_ref.shape, 0)
        col = lax.broadcasted_iota(jnp.int32, band_ref.shape, 1)
        band_ref[...] = jnp.where((col > qi) & (col <= qi + WINDOW), 0.0, NEG_BIG)
        sel_s = lax.broadcasted_iota(jnp.int32, (n_sel, n_cmp), 0) * SEL_LEN
        cmp_c = lax.broadcasted_iota(jnp.int32, (n_sel, n_cmp), 1) * CMP_STRIDE
        selmap_ref[...] = jnp.where((cmp_c < sel_s + SEL_LEN) & (cmp_c + CMP_LEN > sel_s),
                                    1.0, 0.0).astype(BF16)

    q4 = jnp.concatenate([q_ref[0, :, r * hd:(r + 1) * hd] for r in range(R)], axis=0)
    t_rows = s0 + lax.broadcasted_iota(jnp.int32, (rows, 1), 0) % QB

    cmp_end =lax.broadcasted_iota(jnp.int32, (1, n_cmp), 1) * CMP_STRIDE + (CMP_LEN - 1)
    sc = jnp.where(cmp_end <= t_rows, _dot_nt(q4, kc_ref[0, 0]), NEG_BIG)
    mc = jnp.max(sc, axis=-1, keepdims=True)
    pc = jnp.exp(sc - mc)
    lc = jnp.sum(pc, axis=-1, keepdims=True)
    pc = pc * jnp.where(mc > 0.5 * NEG_BIG, 1.0 / jnp.maximum(lc, 1e-30), 0.0)
    o_cmp = _dot(pc.astype(BF16), vc_ref[0, 0])

    pc_sum = pc[0:QB]
    for r in range(1, R):
        pc_sum = pc_sum + pc[r * QB:(r + 1) * QB]
    pc_hi = pc_sum.astype(BF16)
    pc_lo = (pc_sum - pc_hi.astype(F32)).astype(BF16)
    sel_map_t = selmap_ref[...]
    imp =_dot_nt(sel_map_t, pc_hi) + _dot_nt(sel_map_t, pc_lo)

    sid = lax.broadcasted_iota(jnp.int32, (n_sel, QB), 0)
    sid_f = sid.astype(F32)
    tq = s0 + lax.broadcasted_iota(jnp.int32, (n_sel, QB), 1)
    cur = tq // SEL_LEN
    forced = (sid == 0) | (sid == cur) | (sid == cur - 1)
    imp = jnp.where(forced, SEL_FORCE, jnp.where(sid * SEL_LEN <= tq, imp, -SEL_FORCE))

    top_n = min(SEL_TOPN, n_sel)
    sel = jnp.zeros((n_sel, QB), F32)
    for _ in range(top_n):
        best = jnp.max(imp, axis=0, keepdims=True)
        first = jnp.min(jnp.where(imp == best, sid_f, float(n_sel)), axis=0, keepdims=True)
        hit = sid_f == first
        sel = jnp.where(hit, 1.0, sel)
        imp = jnp.where(hit, -jnp.inf, imp)
    bias_t = jnp.where(sel > 0.5, 0.0, NEG_BIG).astype(BF16)
    q_t = [q_ref[0, :, r * hd:(r + 1) * hd].astype(F32).T.astype(BF16) for r in range(R)]
    q_aug_t = jnp.concatenate([jnp.concatenate(q_t, axis=1),
                               jnp.concatenate([bias_t] * R, axis=1)], axis=0)
    t_lanes = s0 + lax.broadcasted_iota(jnp.int32, (1, rows), 1) % QB

    def sel_tile(j, carry, causal):
        m_prev, l_prev, acc = carry
        base = pl.multiple_of(j * SEL_TILE, SEL_TILE)
        s = _dot(kaug_ref[pl.ds(base, SEL_TILE), :], q_aug_t)
        if causal:
            kpos = base + lax.broadcasted_iota(jnp.int32, (SEL_TILE, 1), 0)
            s = jnp.where(kpos <= t_lanes, s, NEG_BIG)
        m_new = jnp.maximum(m_prev, jnp.max(s, axis=0, keepdims=True))
        corr = jnp.exp(m_prev - m_new)
        p = jnp.exp(s - m_new)
        l_new = l_prev * corr + jnp.sum(p, axis=0, keepdims=True)
        acc = acc * corr + _dot(vst_ref[0, 0, :, pl.ds(base, SEL_TILE)], p.astype(BF16))
        return m_new, l_new, acc

    n_past = s0 // SEL_TILE
    init = (jnp.full((1, rows), NEG_BIG, F32), jnp.zeros((1, rows), F32), jnp.zeros((hd, rows), F32))
    carry = lax.fori_loop(0, n_past, functools.partial(sel_tile, causal=False), init)
    _, l_sel, acc_sel = sel_tile(n_past, carry, True)
    o_slc_t = acc_sel * (1.0 / l_sel)
    o_slc = jnp.concatenate([o_slc_t[:, r * QB:(r + 1) * QB].T for r in range(R)], axis=0)

    WB = band_ref.shape[0]
    col = lax.broadcasted_iota(jnp.int32, (1, n_win), 1)
    o_win_sub = []
    for sb in range(QB // WB):
        q_sb = jnp.concatenate([q4[r * QB + sb * WB:r * QB + (sb + 1) * WB] for r in range(R)], axis=0)
        start = pl.multiple_of(s0 + sb * WB, WB)
        win_bias = band_ref[...] + jnp.where(col + start >= WINDOW, 0.0, NEG_BIG)
        sw = _dot_nt(q_sb, kw_ref[0, 0, pl.ds(start, n_win), :])
        sw = jnp.concatenate([sw[r * WB:(r + 1) * WB] + win_bias for r in range(R)], axis=0)
        pw = jnp.exp(sw - jnp.max(sw, axis=-1, keepdims=True))
        lw = jnp.sum(pw, axis=-1, keepdims=True)
        o_win_sub.append(_dot(pw.astype(BF16), vw_ref[0, 0, pl.ds(start, n_win), :]) * (1.0 / lw))
    o_win = jnp.concatenate([o[r * WB:(r + 1) * WB] for r in range(R) for o in o_win_sub], axis=0)

    gates = g_ref[0]
    for r in range(R):
        sl = slice(r * QB, (r + 1) * QB)
        c = r * N_GATES
        out = (gates[:, c:c + 1] * o_cmp[sl] + gates[:, c + 1:c + 2] * o_slc[sl]
               + gates[:, c + 2:c + 3] * o_win[sl])
        o_ref[0, :, r * hd:(r + 1) * hd] = out.astype(o_ref.dtype)


def _nsa_attention(q, gates, kv_cmp, kv_sel, kv_win):
    bsz, seq, _ = q.shape
    G, hd = NSA_KV_HEADS, NSA_HEAD_DIM
    n_cmp = kv_cmp.shape[2]
    gw = NSA_REP * hd
    n_win = WINDOW + WIN_BLOCK
    v_sel_t = jnp.swapaxes(kv_sel[:, G:2 * G], 2, 3)

    def slot(k):
        return lambda b, g, i: (b, k * G + g, 0, 0)

    return pl.pallas_call(
        _nsa_kernel,
        grid=(bsz, G, seq // Q_BLOCK),
        in_specs=[pl.BlockSpec((1, Q_BLOCK, gw), lambda b, g, i: (b, i, g)),
                  pl.BlockSpec((1, Q_BLOCK, LANES), lambda b, g, i: (b, i, g)),
                  pl.BlockSpec((1, 1, n_cmp, hd), slot(0)),
                  pl.BlockSpec((1, 1, n_cmp, hd), slot(1)),
                  pl.BlockSpec((1, 1, seq, hd), slot(0)),
                  pl.BlockSpec((1, 1, hd, seq), lambda b, g, i: (b, g, 0, 0)),
                  pl.BlockSpec((1, 1, WINDOW + seq, hd), slot(0)),
                  pl.BlockSpec((1, 1, WINDOW + seq, hd), slot(1))],
        out_specs=pl.BlockSpec((1, Q_BLOCK, gw), lambda b, g, i: (b, i, g)),
        out_shape=jax.ShapeDtypeStruct(q.shape, BF16),
        scratch_shapes=[pltpu.VMEM((seq, hd + seq // SEL_LEN), BF16),
                        pltpu.VMEM((WIN_BLOCK, n_win), F32),
                        pltpu.VMEM((seq // SEL_LEN, n_cmp), BF16)],
        compiler_params=_params("arbitrary", "arbitrary", "arbitrary"),
        name="nsa_attention",
    )(q, gates, kv_cmp, kv_cmp, kv_sel, v_sel_t, kv_win, kv_win)


def _s5_layer(x, xin, bsz, seq, w_in, tables, d_skip, w_glu, w_out, g, b):
    m = bsz * seq
    u = _mm(xin, w_in, F32, tm=1024, tn=1024)
    y = _s5_core(u.reshape(bsz, seq, D_MODEL), tables, d_skip)
    z = _mm_glu(y.reshape(m, D_MODEL), w_glu)
    return _mm_res_ln(z, w_out, x, g, b)


def _nsa_layer(x, xb, bsz, seq, kv_cmp, kv_sel, kv_win, w_q, w_gate, w_o, g, b):
    m = bsz * seq
    q = _mm(xb, w_q, BF16, tm=1024, tn=1024, scale=NSA_HEAD_DIM ** -0.5)
    gates = _mm(xb, w_gate, F32, tm=1024, tn=512, act="sigmoid")
    o = _nsa_attention(q.reshape(bsz, seq, -1), gates.reshape(bsz, seq, -1), kv_cmp, kv_sel, kv_win)
    return _mm_res_ln(o.reshape(m, -1), w_o, x, g, b)


def _gate_weights(w_qg):
    n_q = NSA_HEADS * NSA_HEAD_DIM
    per = NSA_REP * N_GATES
    wg = w_qg[:, n_q:].reshape(-1, NSA_KV_HEADS, per)
    wg = jnp.pad(wg, ((0, 0), (0, 0), (0, LANES - per)))
    return wg.reshape(-1, NSA_KV_HEADS * LANES).astype(BF16)


def kernel(x, mem, s5_w_in, s5_a_re, s5_a_im, s5_log_dt, s5_b_re, s5_b_im, s5_c_re, s5_c_im, s5_d, s5_w_glu, s5_w_out, kv_w, cmp_pos_k, cmp_w1_k, cmp_w2_k, cmp_pos_v, cmp_w1_v, cmp_w2_v, nsa_w_qg, nsa_w_o, mem_w_q, mem_w_kv, mem_w_o, mlp_w_up, mlp_w_down, ln_g, ln_b):
    bsz, seq, d = x.shape
    m = bsz * seq
    bf = lambda w: w.astype(BF16)
    xf = x.reshape(m, d)
    xin = xf
    memf = mem.reshape(-1, d)
    kv_cmp = kv_sel = kv_win = None
    for layer in range(DEPTH):
        if layer < N_A_LAYERS:
            i = layer
            tables = _s5_tables(s5_a_re[i], s5_a_im[i], s5_log_dt[i], s5_b_re[i], s5_b_im[i],
                                s5_c_re[i], s5_c_im[i], seq // S5_CHUNK)
            xf, xin = _s5_layer(xf, xin, bsz, seq, bf(s5_w_in[i]), tables, s5_d[i], bf(s5_w_glu[i]),
                                bf(s5_w_out[i]), ln_g[layer, 0], ln_b[layer, 0])
        else:
            i = layer - N_A_LAYERS
            if xin.dtype != BF16:
                xin = xin.astype(BF16)
            if i == 0:
                n_cmp_cols = 2 * NSA_KV_HEADS * NSA_HEAD_DIM
                x3 = xin.reshape(bsz, seq, d)
                kv_c = _kv_proj(x3, bf(kv_w[:, :n_cmp_cols]), F32)
                kv_sel = _kv_proj(x3, bf(kv_w[:, n_cmp_cols:]), BF16)
                kv_win = jnp.pad(kv_sel[:, 2 * NSA_KV_HEADS:], ((0, 0), (0, 0), (WINDOW, 0), (0, 0)))
                half = CMP_STRIDE * NSA_HEAD_DIM
                pos = jnp.stack([cmp_pos_k.reshape(2, half), cmp_pos_v.reshape(2, half)])
                kv_cmp = _compress(kv_c, pos, jnp.stack([bf(cmp_w1_k), bf(cmp_w1_v)]),
                                   jnp.stack([bf(cmp_w2_k), bf(cmp_w2_v)]))
            n_q = NSA_HEADS * NSA_HEAD_DIM
            xf, xin = _nsa_layer(xf, xin, bsz, seq, kv_cmp, kv_sel, kv_win, bf(nsa_w_qg[i][:, :n_q]),
                                 _gate_weights(nsa_w_qg[i]), bf(nsa_w_o[i]),
                                 ln_g[layer, 0], ln_b[layer, 0])
        kvm = _mm(memf, bf(mem_w_kv[layer]), BF16, tm=512, tn=1024)
        o = _mem_attn(xin.reshape(bsz, seq, d), bf(mem_w_q[layer]), kvm.reshape(bsz, -1, 2 * d))
        xf, xin = _mm_res_ln(o.reshape(m, d), bf(mem_w_o[layer]), xf, ln_g[layer, 1], ln_b[layer, 1])
        xf, xin = _mlp_ln(xin, xf, bf(mlp_w_up[layer]), bf(mlp_w_down[layer]),
                          ln_g[layer, 2], ln_b[layer, 2])
    return xf.reshape(bsz, seq, d)
```
